```python
import math
import jax, jax.numpy as jnp
from jax import lax
import numpy as np

D_MODEL = 1024
BATCH = 2
SEQ = 8192
DEPTH = 1

D_POOL = D_MODEL
POOL_WINDOWS = (2, 4, 8, 16)
POOL_GROUPS = len(POOL_WINDOWS)
POOL_GROUP_WIDTH = D_POOL // POOL_GROUPS
D_SSM = D_MODEL
SSM_GROUP = 16
SSM_GROUPS = D_SSM // SSM_GROUP
SSM_STATE = 64
DT_MIN = 1e-3
DT_MAX = 1e-1
RMS_EPS = 1e-6
IN_SPLITS = (D_POOL, 2 * D_POOL, 2 * D_POOL + D_SSM, 2 * D_POOL + 2 * D_SSM,
             2 * D_POOL + 2 * D_SSM + D_MODEL)
IN_COLS = 2 * D_POOL + 2 * D_SSM + 2 * D_MODEL

kernel_name = 'hybrid_pool_s5_gated_block'


def rmsnorm(x, gain):
    xf = x.astype(jnp.float32)
    y = xf * lax.rsqrt(jnp.mean(xf * xf, axis=-1, keepdims=True) + RMS_EPS)
    return (y * gain.astype(jnp.float32)).astype(x.dtype)


def causal_pool_mixer(u, pool_w, pool_scale):
    bsz, seq, _ = u.shape
    uf = u.astype(jnp.float32)
    csum = jnp.cumsum(uf, axis=1)
    pos = jnp.arange(1, seq + 1, dtype=jnp.float32)[None, :, None]
    pooled = []
    for g, w in enumerate(POOL_WINDOWS):
        cs = csum[..., g * POOL_GROUP_WIDTH:(g + 1) * POOL_GROUP_WIDTH]
        lagged = jnp.pad(cs, ((0, 0), (w, 0), (0, 0)))[:, :seq]
        count = jnp.minimum(pos, float(w))
        pooled.append((cs - lagged) / count)
    pooled = (jnp.concatenate(pooled, axis=-1) - uf).astype(u.dtype)
    pooled = pooled.reshape(bsz, seq, POOL_GROUPS, POOL_GROUP_WIDTH)
    mixed = jnp.einsum('blgc,gcd->blgd', pooled, pool_w).reshape(bsz, seq, D_POOL)
    return mixed * pool_scale


def _complex_linear_combine(e1, e2):
    ar1, ai1, br1, bi1 = e1
    ar2, ai2, br2, bi2 = e2
    ar = ar2 * ar1 - ai2 * ai1
    ai = ar2 * ai1 + ai2 * ar1
    br = ar2 * br1 - ai2 * bi1 + br2
    bi = ar2 * bi1 + ai2 * br1 + bi2
    return (ar, ai, br, bi)


def s5_mixer(u, a_re, a_im, log_dt, b_re, b_im, c_re, c_im, d_skip, glu_w, glu_b):
    bsz, seq, _ = u.shape
    uf = u.astype(jnp.float32).reshape(bsz, seq, SSM_GROUPS, SSM_GROUP)
    dt = jnp.exp(log_dt.astype(jnp.float32))[:, None]
    lam_re = jnp.minimum(a_re.astype(jnp.float32), -1e-4)
    lam_im = a_im.astype(jnp.float32)
    mag = jnp.exp(lam_re * dt)
    abar_re = mag * jnp.cos(lam_im * dt)
    abar_im = mag * jnp.sin(lam_im * dt)
    den = lam_re * lam_re + lam_im * lam_im
    num_re = abar_re - 1.0
    f_re = (num_re * lam_re + abar_im * lam_im) / den
    f_im = (abar_im * lam_re - num_re * lam_im) / den
    f_re, f_im = f_re[:, :, None], f_im[:, :, None]
    bb_re = f_re * b_re - f_im * b_im
    bb_im = f_re * b_im + f_im * b_re
    bu_re = jnp.einsum('blgh,gph->blgp', uf, bb_re)
    bu_im = jnp.einsum('blgh,gph->blgp', uf, bb_im)
    a_seq_re = jnp.broadcast_to(abar_re[None, None], (1, seq, SSM_GROUPS, SSM_STATE))
    a_seq_im = jnp.broadcast_to(abar_im[None, None], (1, seq, SSM_GROUPS, SSM_STATE))
    _, _, s_re, s_im = lax.associative_scan(
        _complex_linear_combine, (a_seq_re, a_seq_im, bu_re, bu_im), axis=1)
    y = (jnp.einsum('blgp,ghp->blgh', s_re, c_re)
         - jnp.einsum('blgp,ghp->blgh', s_im, c_im)
         + d_skip * uf)
    y = jax.nn.gelu(y.reshape(bsz, seq, D_SSM).astype(u.dtype))
    return y * jax.nn.sigmoid(y @ glu_w + glu_b)


def hybrid_layer(x, c, w_ada, b_ada, norm_pre, norm_post, w_in, pool_w, pool_scale,
                 a_re, a_im, log_dt, b_re, b_im, c_re, c_im, d_skip, glu_w, glu_b,
                 w_branch_pool, w_branch_ssm, w_out):
    mod = jax.nn.silu(c) @ w_ada + b_ada
    shift, scale, gate = jnp.split(mod, 3, axis=-1)
    h = rmsnorm(x, norm_pre) * (1.0 + scale[:, None, :]) + shift[:, None, :]
    proj = h @ w_in
    u_pool, z_pool, u_ssm, z_ssm, g_pool, g_ssm = jnp.split(proj, list(IN_SPLITS), axis=-1)
    y_pool = causal_pool_mixer(u_pool, pool_w, pool_scale) * jax.nn.silu(z_pool)
    y_ssm = s5_mixer(u_ssm, a_re, a_im, log_dt, b_re, b_im, c_re, c_im, d_skip,
                     glu_w, glu_b) * jax.nn.silu(z_ssm)
    merged = (jax.nn.sigmoid(g_pool) * (y_pool @ w_branch_pool)
              + jax.nn.sigmoid(g_ssm) * (y_ssm @ w_branch_ssm))
    out = merged @ w_out
    return x + gate[:, None, :] * rmsnorm(out, norm_post)


def setup_inputs(seed: int = 0) -> dict:
    key = jax.random.key(seed)
    ks = jax.random.split(key, 24)
    f32 = jnp.float32

    def nrm(k, shape, s):
        return jax.random.normal(k, shape, f32) * s

    G, P, H = SSM_GROUPS, SSM_STATE, SSM_GROUP
    n_idx = jnp.arange(P, dtype=f32)
    return {
        'x': nrm(ks[0], (BATCH, SEQ, D_MODEL), 1.0),
        'c': nrm(ks[1], (BATCH, D_MODEL), 1.0),
        'w_ada': nrm(ks[2], (DEPTH, D_MODEL, 3 * D_MODEL), 0.5 * D_MODEL ** -0.5),
        'b_ada': nrm(ks[3], (DEPTH, 3 * D_MODEL), 0.02),
        'norm_pre': 1.0 + nrm(ks[4], (DEPTH, D_MODEL), 0.02),
        'norm_post': 1.0 + nrm(ks[5], (DEPTH, D_MODEL), 0.02),
        'w_in': nrm(ks[6], (DEPTH, D_MODEL, IN_COLS), D_MODEL ** -0.5),
        'pool_w': nrm(ks[7], (DEPTH, POOL_GROUPS, POOL_GROUP_WIDTH, POOL_GROUP_WIDTH),
                      POOL_GROUP_WIDTH ** -0.5),
        'pool_scale': 1.0 + nrm(ks[8], (DEPTH, D_POOL), 0.02),
        'ssm_a_re': -0.5 + nrm(ks[9], (DEPTH, G, P), 0.01),
        'ssm_a_im': math.pi * n_idx + nrm(ks[10], (DEPTH, G, P), 0.01),
        'ssm_log_dt': jax.random.uniform(ks[11], (DEPTH, G), f32,
                                         math.log(DT_MIN), math.log(DT_MAX)),
        'ssm_b_re': nrm(ks[12], (DEPTH, G, P, H), (2 * H) ** -0.5),
        'ssm_b_im': nrm(ks[13], (DEPTH, G, P, H), (2 * H) ** -0.5),
        'ssm_c_re': nrm(ks[14], (DEPTH, G, H, P), P ** -0.5),
        'ssm_c_im': nrm(ks[15], (DEPTH, G, H, P), P ** -0.5),
        'ssm_d': nrm(ks[16], (DEPTH, G, H), 0.5),
        'glu_w': nrm(ks[17], (DEPTH, D_SSM, D_SSM), D_SSM ** -0.5),
        'glu_b': nrm(ks[18], (DEPTH, D_SSM), 0.02),
        'w_branch_pool': nrm(ks[19], (DEPTH, D_POOL, D_MODEL), D_POOL ** -0.5),
        'w_branch_ssm': nrm(ks[20], (DEPTH, D_SSM, D_MODEL), D_SSM ** -0.5),
        'w_out': nrm(ks[21], (DEPTH, D_MODEL, D_MODEL), D_MODEL ** -0.5),
    }


def reference(x, c, w_ada, b_ada, norm_pre, norm_post, w_in, pool_w, pool_scale,
              ssm_a_re, ssm_a_im, ssm_log_dt, ssm_b_re, ssm_b_im, ssm_c_re, ssm_c_im,
              ssm_d, glu_w, glu_b, w_branch_pool, w_branch_ssm, w_out):
    for layer in range(DEPTH):
        x = hybrid_layer(x, c, w_ada[layer], b_ada[layer], norm_pre[layer], norm_post[layer],
                         w_in[layer], pool_w[layer], pool_scale[layer],
                         ssm_a_re[layer], ssm_a_im[layer], ssm_log_dt[layer],
                         ssm_b_re[layer], ssm_b_im[layer], ssm_c_re[layer], ssm_c_im[layer],
                         ssm_d[layer], glu_w[layer], glu_b[layer],
                         w_branch_pool[layer], w_branch_ssm[layer], w_out[layer])
    return x
```

```python
import functools
import math

import numpy as np
import jax
import jax.numpy as jnp
from jax import lax
from jax.experimental import pallas as pl
from jax.experimental.pallas import tpu as pltpu

D_MODEL = 1024
POOL_WINDOWS = (2, 4, 8, 16)
POOL_GROUP_WIDTH = D_MODEL // len(POOL_WINDOWS)
POOL_HALO = 16
SSM_GROUP = 16
SSM_GROUPS = D_MODEL // SSM_GROUP
SSM_STATE = 64
RMS_EPS = 1e-6

SUBLANES = 8
MXU_DIM = 256
GROUPS_PER_KBLOCK = MXU_DIM // SSM_GROUP
N_KBLOCKS = D_MODEL // MXU_DIM
KBLOCK_STATES = GROUPS_PER_KBLOCK * SSM_STATE
BLOCK_T = 256
SCAN_STEPS = BLOCK_T // SUBLANES
VMEM_LIMIT_BYTES = 56 * 1024 * 1024

_POW_STEP, _POW_S1, _POW_S2, _POW_S4, _POW_CARRY = 0, 1, 2, 3, 4
_N_POW = 5


def _sigmoid(v):
    return 0.5 * jnp.tanh(0.5 * v) + 0.5


def _silu(v):
    return v * _sigmoid(v)


def _gelu_tanh(v):
    c = math.sqrt(2.0 / math.pi)
    return 0.5 * v * (1.0 + jnp.tanh(c * (v + 0.044715 * (v * v * v))))


def _dot(a, b):
    return jnp.dot(a, b, preferred_element_type=jnp.float32)


def _ssm_prep_kernel(are_ref, aim_ref, ldt_ref, arer_ref, aimr_ref, bre_ref, bim_ref,
                     bbre_ref, bbim_ref, pwre_ref, pwim_ref):
    dt = jnp.exp(ldt_ref[...])
    lam_re = jnp.minimum(arer_ref[...], -1e-4)
    lam_im = aimr_ref[...]
    mag = jnp.exp(lam_re * dt)
    abar_re = mag * jnp.cos(lam_im * dt)
    abar_im = mag * jnp.sin(lam_im * dt)
    den = lam_re * lam_re + lam_im * lam_im
    num_re = abar_re - 1.0
    f_re = (num_re * lam_re + abar_im * lam_im) / den
    f_im = (abar_im * lam_re - num_re * lam_im) / den
    b_re = bre_ref[...]
    b_im = bim_ref[...]
    bbre_ref[...] = f_re * b_re - f_im * b_im
    bbim_ref[...] = f_re * b_im + f_im * b_re
    l_re = jnp.minimum(are_ref[...], -1e-4) * dt
    l_im = aim_ref[...] * dt
    powers = ([1, SCAN_STEPS, 2 * SCAN_STEPS, 4 * SCAN_STEPS]
              + [SCAN_STEPS * (q + 1) for q in range(SUBLANES)])
    for idx, n in enumerate(powers):
        m = jnp.exp(float(n) * l_re)
        pwre_ref[idx] = m * jnp.cos(float(n) * l_im)
        pwim_ref[idx] = m * jnp.sin(float(n) * l_im)


def _ssm_prep(a_re, a_im, log_dt, b_re, b_im):
    g, p, h = SSM_GROUPS, SSM_STATE, SSM_GROUP
    n_pow = 4 + SUBLANES
    f32 = jnp.float32
    return pl.pallas_call(
        _ssm_prep_kernel,
        out_shape=(jax.ShapeDtypeStruct((g, p * h), f32), jax.ShapeDtypeStruct((g, p * h), f32),
                   jax.ShapeDtypeStruct((n_pow, g, p), f32), jax.ShapeDtypeStruct((n_pow, g, p), f32)),
        name="ssm_prep",
    )(a_re, a_im, log_dt.reshape(g, 1), jnp.repeat(a_re, h, axis=1), jnp.repeat(a_im, h, axis=1),
      b_re.reshape(g, p * h), b_im.reshape(g, p * h))


def _adaln_kernel(c_ref, w_ref, b_ref, o_ref):
    o_ref[...] = _dot(_silu(c_ref[...]), w_ref[...]) + b_ref[...]


def _adaln(c_pad, w_ada, b_ada):
    rows = c_pad.shape[0]
    return pl.pallas_call(
        _adaln_kernel,
        grid=(3,),
        in_specs=[pl.BlockSpec((rows, D_MODEL), lambda j: (0, 0)),
                  pl.BlockSpec((D_MODEL, D_MODEL), lambda j: (0, j)),
                  pl.BlockSpec((1, D_MODEL), lambda j: (0, j))],
        out_specs=pl.BlockSpec((rows, D_MODEL), lambda j: (0, j)),
        out_shape=jax.ShapeDtypeStruct((rows, 3 * D_MODEL), jnp.float32),
        name="adaln",
    )(c_pad, w_ada, b_ada.reshape(1, 3 * D_MODEL))


def _cmul_add(ar, ai, xr, xi, br, bi):
    return ar * xr - ai * xi + br, ar * xi + ai * xr + bi


def _block_kernel(x_ref, mod_ref, npre_ref, npost_ref, wnat_ref, wssm_ref, perm_ref, permt_ref,
                  poolw_ref, pscale_ref, pow_ref, bblk_ref, cblk_ref, dskip_ref, gluw_ref,
                  glub_ref, wbp_ref, wbs_ref, wout_ref, o_ref,
                  ext_ref, bu_ref, xs_ref, carry_ref):
    j = pl.program_id(1)
    bf16 = jnp.bfloat16
    kb = KBLOCK_STATES

    @pl.when(j == 0)
    def _():
        ext_ref[0:POOL_HALO, :] = jnp.zeros((POOL_HALO, D_MODEL), jnp.float32)
        carry_ref[...] = jnp.zeros_like(carry_ref)

    x = x_ref[0]
    shift = mod_ref[0, 0:1, :]
    scale = mod_ref[0, 1:2, :]
    gate = mod_ref[0, 2:3, :]

    xn = x * lax.rsqrt(jnp.mean(x * x, axis=-1, keepdims=True) + RMS_EPS) * npre_ref[...]
    hb = (xn * (1.0 + scale) + shift).astype(bf16)

    pn = _dot(hb, wnat_ref[...])
    u_pool = pn[:, 0:D_MODEL]

    ext_ref[POOL_HALO:, :] = u_pool
    pos = (j * BLOCK_T + 1 + lax.broadcasted_iota(jnp.int32, (BLOCK_T, 1), 0)).astype(jnp.float32)
    mixed = []
    for g, w in enumerate(POOL_WINDOWS):
        cols = slice(g * POOL_GROUP_WIDTH, (g + 1) * POOL_GROUP_WIDTH)
        wsum = u_pool[:, cols]
        for lag in range(1, w):
            wsum = wsum + ext_ref[POOL_HALO - lag:POOL_HALO - lag + BLOCK_T, cols]
        pooled = wsum / jnp.minimum(pos, float(w)) - u_pool[:, cols]
        mixed.append(_dot(pooled.astype(bf16), poolw_ref[g]))
    ext_ref[0:POOL_HALO, :] = ext_ref[BLOCK_T:BLOCK_T + POOL_HALO, :]
    y_pool = jnp.concatenate(mixed, axis=-1) * pscale_ref[...] * _silu(pn[:, D_MODEL:2 * D_MODEL])

    hp = _dot(perm_ref[...], hb).astype(bf16)
    ps = _dot(hp, wssm_ref[...])
    u_ssm = ps[:, 0:D_MODEL]
    ub = u_ssm.astype(bf16)

    row = lax.broadcasted_iota(jnp.int32, (SUBLANES, kb), 0)
    y_parts = []
    for k in range(N_KBLOCKS):
        bu_ref[...] = _dot(ub[:, k * MXU_DIM:(k + 1) * MXU_DIM], bblk_ref[k])
        ar = pow_ref[k, _POW_STEP, 0]
        ai = pow_ref[k, _POW_STEP, 1]

        def scan_step(i, xr, xi):
            r0 = pl.multiple_of(i * SUBLANES, SUBLANES)
            return _cmul_add(ar, ai, xr, xi, bu_ref[pl.ds(r0, SUBLANES), 0:kb],
                             bu_ref[pl.ds(r0, SUBLANES), kb:2 * kb])

        zero = jnp.zeros((SUBLANES, kb), jnp.float32)
        er, ei = lax.fori_loop(0, SCAN_STEPS, lambda i, c: scan_step(i, *c), (zero, zero), unroll=4)
        for idx, sh in ((_POW_S1, 1), (_POW_S2, 2), (_POW_S4, 4)):
            sr = jnp.where(row >= sh, pltpu.roll(er, sh, 0), 0.0)
            si = jnp.where(row >= sh, pltpu.roll(ei, sh, 0), 0.0)
            er, ei = _cmul_add(pow_ref[k, idx, 0], pow_ref[k, idx, 1], sr, si, er, ei)
        cr = carry_ref[k, 0]
        ci = carry_ref[k, 1]
        er, ei = _cmul_add(pow_ref[k, _POW_CARRY, 0], pow_ref[k, _POW_CARRY, 1], cr, ci, er, ei)
        carry_ref[k, 0] = jnp.broadcast_to(er[SUBLANES - 1:SUBLANES], (SUBLANES, kb))
        carry_ref[k, 1] = jnp.broadcast_to(ei[SUBLANES - 1:SUBLANES], (SUBLANES, kb))
        x0r = jnp.where(row == 0, cr, pltpu.roll(er, 1, 0))
        x0i = jnp.where(row == 0, ci, pltpu.roll(ei, 1, 0))

        def emit_step(i2, c):
            xr, xi = c
            xr1, xi1 = scan_step(2 * i2, xr, xi)
            xr2, xi2 = scan_step(2 * i2 + 1, xr1, xi1)
            r0 = pl.multiple_of(i2 * 2 * SUBLANES, 2 * SUBLANES)
            xs_ref[pl.ds(r0, 2 * SUBLANES), 0:kb] = jnp.concatenate([xr1, xr2], axis=0).astype(bf16)
            xs_ref[pl.ds(r0, 2 * SUBLANES), kb:2 * kb] = jnp.concatenate([xi1, xi2], axis=0).astype(bf16)
            return xr2, xi2

        lax.fori_loop(0, SCAN_STEPS // 2, emit_step, (x0r, x0i), unroll=2)
        y_parts.append(_dot(xs_ref[...], cblk_ref[k]))

    y = _gelu_tanh(jnp.concatenate(y_parts, axis=-1) + dskip_ref[...] * u_ssm)
    y = y * _sigmoid(_dot(y.astype(bf16), gluw_ref[...]) + glub_ref[...])
    y_ssm = (y * _silu(ps[:, D_MODEL:2 * D_MODEL])).astype(bf16)
    y_ssm = _dot(permt_ref[...], y_ssm).astype(bf16)

    merged = (_sigmoid(pn[:, 2 * D_MODEL:3 * D_MODEL]) * _dot(y_pool.astype(bf16), wbp_ref[...])
              + _sigmoid(pn[:, 3 * D_MODEL:4 * D_MODEL]) * _dot(y_ssm, wbs_ref[...]))
    out = _dot(merged.astype(bf16), wout_ref[...])
    rn = out * lax.rsqrt(jnp.mean(out * out, axis=-1, keepdims=True) + RMS_EPS) * npost_ref[...]
    o_ref[0] = x + gate * rn


def _const_spec(shape):
    nd = len(shape)
    return pl.BlockSpec(shape, lambda b, j: (0,) * nd, pipeline_mode=pl.Buffered(1))


def _scan_permutation():
    p = np.zeros((BLOCK_T, BLOCK_T), np.float32)
    for i in range(SCAN_STEPS):
        for q in range(SUBLANES):
            p[SUBLANES * i + q, q * SCAN_STEPS + i] = 1.0
    return p


def _block_diag_in(bb):
    v = bb.reshape(N_KBLOCKS, GROUPS_PER_KBLOCK, SSM_STATE, SSM_GROUP).transpose(0, 1, 3, 2)
    eye = jnp.eye(GROUPS_PER_KBLOCK, dtype=bb.dtype)
    dense = v[:, :, :, None, :] * eye[None, :, None, :, None]
    return dense.reshape(N_KBLOCKS, MXU_DIM, KBLOCK_STATES)


def _block_diag_out(c):
    v = c.reshape(N_KBLOCKS, GROUPS_PER_KBLOCK, SSM_GROUP, SSM_STATE).transpose(0, 1, 3, 2)
    eye = jnp.eye(GROUPS_PER_KBLOCK, dtype=c.dtype)
    dense = v[:, :, :, None, :] * eye[None, :, None, :, None]
    return dense.reshape(N_KBLOCKS, KBLOCK_STATES, MXU_DIM)


def _layer(x, c, w_ada, b_ada, norm_pre, norm_post, w_in, pool_w, pool_scale, a_re, a_im, log_dt,
           b_re, b_im, c_re, c_im, d_skip, glu_w, glu_b, w_branch_pool, w_branch_ssm, w_out):
    bsz, seq, d = x.shape
    assert d == D_MODEL and seq % BLOCK_T == 0
    bf16 = jnp.bfloat16
    f32 = jnp.float32

    bb_re, bb_im, pw_re, pw_im = _ssm_prep(a_re, a_im, log_dt, b_re, b_im)
    c_pad = jnp.zeros((SUBLANES, d), f32).at[:bsz].set(c)
    mod = _adaln(c_pad, w_ada, b_ada)[:bsz].reshape(bsz, 3, d)

    bblk = jnp.concatenate([_block_diag_in(bb_re), _block_diag_in(bb_im)], axis=-1).astype(bf16)
    cblk = jnp.concatenate([_block_diag_out(c_re), _block_diag_out(-c_im)], axis=1).astype(bf16)
    n_pow = pw_re.shape[0]
    pw = jnp.stack([pw_re, pw_im], axis=1).reshape(n_pow, 2, N_KBLOCKS, 1, KBLOCK_STATES)
    step_rows = jnp.broadcast_to(pw[:4], (4, 2, N_KBLOCKS, SUBLANES, KBLOCK_STATES))
    carry_rows = pw[4:, :, :, 0, :].transpose(1, 2, 0, 3)[None]
    pow_tab = jnp.concatenate([step_rows, carry_rows], axis=0).transpose(2, 0, 1, 3, 4)
    w_nat = jnp.concatenate([w_in[:, 0:2 * d], w_in[:, 4 * d:6 * d]], axis=1).astype(bf16)
    w_ssm = w_in[:, 2 * d:4 * d].astype(bf16)
    perm = _scan_permutation()

    row = lambda v: v.reshape(1, d)
    operands = [
        (x, pl.BlockSpec((1, BLOCK_T, d), lambda b, j: (b, j, 0))),
        (mod, pl.BlockSpec((1, 3, d), lambda b, j: (b, 0, 0))),
        (row(norm_pre), None), (row(norm_post), None),
        (w_nat, None), (w_ssm, None),
        (jnp.asarray(perm, bf16), None), (jnp.asarray(perm.T, bf16), None),
        (pool_w.astype(bf16), None), (row(pool_scale), None),
        (pow_tab, None), (bblk, None), (cblk, None), (row(d_skip), None),
        (glu_w.astype(bf16), None), (row(glu_b), None),
        (w_branch_pool.astype(bf16), None), (w_branch_ssm.astype(bf16), None),
        (w_out.astype(bf16), None),
    ]
    arrays = [a for a, _ in operands]
    specs = [s if s is not None else _const_spec(a.shape) for a, s in operands]

    return pl.pallas_call(
        _block_kernel,
        grid=(bsz, seq // BLOCK_T),
        in_specs=specs,
        out_specs=pl.BlockSpec((1, BLOCK_T, d), lambda b, j: (b, j, 0)),
        out_shape=jax.ShapeDtypeStruct((bsz, seq, d), x.dtype),
        scratch_shapes=[
            pltpu.VMEM((BLOCK_T + POOL_HALO, d), f32),
            pltpu.VMEM((BLOCK_T, 2 * KBLOCK_STATES), f32),
            pltpu.VMEM((BLOCK_T, 2 * KBLOCK_STATES), bf16),
            pltpu.VMEM((N_KBLOCKS, 2, SUBLANES, KBLOCK_STATES), f32),
        ],
        compiler_params=pltpu.CompilerParams(
            dimension_semantics=("arbitrary", "arbitrary"),
            vmem_limit_bytes=VMEM_LIMIT_BYTES),
        name="block",
    )(*arrays)


def kernel(x, c, w_ada, b_ada, norm_pre, norm_post, w_in, pool_w, pool_scale, ssm_a_re, ssm_a_im,
           ssm_log_dt, ssm_b_re, ssm_b_im, ssm_c_re, ssm_c_im, ssm_d, glu_w, glu_b, w_branch_pool,
           w_branch_ssm, w_out):
    for layer in range(w_in.shape[0]):
        x = _layer(x, c, w_ada[layer], b_ada[layer], norm_pre[layer], norm_post[layer],
                   w_in[layer], pool_w[layer], pool_scale[layer], ssm_a_re[layer],
                   ssm_a_im[layer], ssm_log_dt[layer], ssm_b_re[layer], ssm_b_im[layer],
                   ssm_c_re[layer], ssm_c_im[layer], ssm_d.reshape(ssm_d.shape[0], -1)[layer],
                   glu_w[layer], glu_b[layer], w_branch_pool[layer], w_branch_ssm[layer],
                   w_out[layer])
    return x
```

```python
import functools
import math

import numpy as np
import jax
import jax.numpy as jnp
from jax import lax
from jax.experimental import pallas as pl
from jax.experimental.pallas import tpu as pltpu

D_MODEL = 1024
POOL_WINDOWS = (2, 4, 8, 16)
POOL_GROUP_WIDTH = D_MODEL // len(POOL_WINDOWS)
POOL_HALO = 16
SSM_GROUP = 16
SSM_GROUPS = D_MODEL // SSM_GROUP
SSM_STATE = 64
RMS_EPS = 1e-6

SUBLANES = 8
MXU_DIM = 256
GROUPS_PER_KBLOCK = MXU_DIM // SSM_GROUP
N_KBLOCKS = D_MODEL // MXU_DIM
KBLOCK_STATES = GROUPS_PER_KBLOCK * SSM_STATE
BLOCK_T = 256
SCAN_STEPS = BLOCK_T // SUBLANES
VMEM_LIMIT_BYTES = 56 * 1024 * 1024

_POW_STEP, _POW_S1, _POW_S2, _POW_S4, _POW_CARRY = 0, 1, 2, 3, 4
_N_POW = 5


def _sigmoid(v):
    return 0.5 * jnp.tanh(0.5 * v) + 0.5


def _silu(v):
    return v * _sigmoid(v)


def _gelu_tanh(v):
    c = math.sqrt(2.0 / math.pi)
    return 0.5 * v * (1.0 + jnp.tanh(c * (v + 0.044715 * (v * v * v))))


def _dot(a, b):
    return jnp.dot(a, b, preferred_element_type=jnp.float32)


def _ssm_prep_kernel(are_ref, aim_ref, ldt_ref, bre_ref, bim_ref, cre_ref, cim_ref,
                     bblk_ref, cblk_ref, pow_ref):
    bf16 = jnp.bfloat16
    kb = KBLOCK_STATES
    dt = jnp.exp(ldt_ref[...])
    lam_re = jnp.minimum(are_ref[...], -1e-4)
    lam_im = aim_ref[...]
    l_re = lam_re * dt
    l_im = lam_im * dt

    def abar_pow(n):
        m = jnp.exp(float(n) * l_re)
        return m * jnp.cos(float(n) * l_im), m * jnp.sin(float(n) * l_im)

    abar_re, abar_im = abar_pow(1)
    den = lam_re * lam_re + lam_im * lam_im
    num_re = abar_re - 1.0
    f_re = (num_re * lam_re + abar_im * lam_im) / den
    f_im = (abar_im * lam_re - num_re * lam_im) / den
    b_re = bre_ref[...]
    b_im = bim_ref[...]
    bb = (f_re * b_re - f_im * b_im, f_re * b_im + f_im * b_re)

    in_mask = (lax.broadcasted_iota(jnp.int32, (MXU_DIM, kb), 0) // SSM_GROUP
               == lax.broadcasted_iota(jnp.int32, (MXU_DIM, kb), 1) // SSM_STATE)
    out_mask = (lax.broadcasted_iota(jnp.int32, (kb, MXU_DIM), 0) // SSM_STATE
                == lax.broadcasted_iota(jnp.int32, (kb, MXU_DIM), 1) // SSM_GROUP)
    spread = (lax.broadcasted_iota(jnp.int32, (SSM_GROUP, MXU_DIM), 0)
              == lax.broadcasted_iota(jnp.int32, (SSM_GROUP, MXU_DIM), 1) % SSM_GROUP).astype(bf16)
    for k in range(N_KBLOCKS):
        cols = slice(k * kb, (k + 1) * kb)
        for ri in range(2):
            tiled = jnp.concatenate([bb[ri][:, cols]] * GROUPS_PER_KBLOCK, axis=0)
            bblk_ref[k, :, ri * kb:(ri + 1) * kb] = jnp.where(in_mask, tiled, 0.0).astype(bf16)
            c_src = (cre_ref if ri == 0 else cim_ref)[cols, :].astype(bf16)
            c_rep = _dot(c_src, spread)
            c_rep = c_rep if ri == 0 else -c_rep
            cblk_ref[k, ri * kb:(ri + 1) * kb, :] = jnp.where(out_mask, c_rep, 0.0).astype(bf16)

    for idx, n in ((_POW_STEP, 1), (_POW_S1, SCAN_STEPS), (_POW_S2, 2 * SCAN_STEPS),
                   (_POW_S4, 4 * SCAN_STEPS)):
        pw = abar_pow(n)
        for k in range(N_KBLOCKS):
            for ri in range(2):
                pow_ref[k, idx, ri] = jnp.broadcast_to(pw[ri][:, k * kb:(k + 1) * kb], (SUBLANES, kb))
    for q in range(SUBLANES):
        pw = abar_pow(SCAN_STEPS * (q + 1))
        for k in range(N_KBLOCKS):
            for ri in range(2):
                pow_ref[k, _POW_CARRY, ri, q:q + 1, :] = pw[ri][:, k * kb:(k + 1) * kb]


def _ssm_prep(a_re, a_im, log_dt, b_re, b_im, c_re, c_im):
    g, p, h = SSM_GROUPS, SSM_STATE, SSM_GROUP
    flat = lambda v: v.reshape(1, g * p)
    return pl.pallas_call(
        _ssm_prep_kernel,
        out_shape=(jax.ShapeDtypeStruct((N_KBLOCKS, MXU_DIM, 2 * KBLOCK_STATES), jnp.bfloat16),
                   jax.ShapeDtypeStruct((N_KBLOCKS, 2 * KBLOCK_STATES, MXU_DIM), jnp.bfloat16),
                   jax.ShapeDtypeStruct((N_KBLOCKS, _N_POW, 2, SUBLANES, KBLOCK_STATES),
                                        jnp.float32)),
        compiler_params=pltpu.CompilerParams(vmem_limit_bytes=VMEM_LIMIT_BYTES),
        name="ssm_prep",
    )(flat(a_re), flat(a_im), flat(jnp.repeat(log_dt, p)),
      b_re.reshape(g * p, h).T, b_im.reshape(g * p, h).T,
      c_re.transpose(0, 2, 1).reshape(g * p, h), c_im.transpose(0, 2, 1).reshape(g * p, h))


def _adaln_kernel(c_ref, w_ref, b_ref, o_ref):
    o_ref[...] = _dot(_silu(c_ref[...]), w_ref[...]) + b_ref[...]


def _adaln(c_pad, w_ada, b_ada):
    rows = c_pad.shape[0]
    return pl.pallas_call(
        _adaln_kernel,
        grid=(3,),
        in_specs=[pl.BlockSpec((rows, D_MODEL), lambda j: (0, 0)),
                  pl.BlockSpec((D_MODEL, D_MODEL), lambda j: (0, j)),
                  pl.BlockSpec((1, D_MODEL), lambda j: (0, j))],
        out_specs=pl.BlockSpec((rows, D_MODEL), lambda j: (0, j)),
        out_shape=jax.ShapeDtypeStruct((rows, 3 * D_MODEL), jnp.float32),
        name="adaln",
    )(c_pad, w_ada, b_ada.reshape(1, 3 * D_MODEL))


def _cmul_add(ar, ai, xr, xi, br, bi):
    return ar * xr - ai * xi + br, ar * xi + ai * xr + bi


def _block_kernel(x_ref, mod_ref, npre_ref, npost_ref, win_ref, perm_ref, permt_ref,
                  poolw_ref, pscale_ref, pow_ref, bblk_ref, cblk_ref, dskip_ref, gluw_ref,
                  glub_ref, wbp_ref, wbs_ref, wout_ref, o_ref,
                  ext_ref, bu2_ref, xs2_ref, carry_ref):
    j = pl.program_id(1)
    bf16 = jnp.bfloat16
    kb = KBLOCK_STATES

    @pl.when(j == 0)
    def _():
        ext_ref[0:POOL_HALO, :] = jnp.zeros((POOL_HALO, D_MODEL), jnp.float32)
        carry_ref[...] = jnp.zeros_like(carry_ref)

    x = x_ref[0]
    shift = mod_ref[0, 0:1, :]
    scale = mod_ref[0, 1:2, :]
    gate = mod_ref[0, 2:3, :]

    xn = x * lax.rsqrt(jnp.mean(x * x, axis=-1, keepdims=True) + RMS_EPS) * npre_ref[...]
    hb = (xn * (1.0 + scale) + shift).astype(bf16)

    pn = _dot(hb, win_ref[:, 0:2 * D_MODEL])
    gates = _dot(hb, win_ref[:, 4 * D_MODEL:6 * D_MODEL])
    u_pool = pn[:, 0:D_MODEL]

    ext_ref[POOL_HALO:, :] = u_pool
    pos = (j * BLOCK_T + 1 + lax.broadcasted_iota(jnp.int32, (BLOCK_T, 1), 0)).astype(jnp.float32)
    mixed = []
    for g, w in enumerate(POOL_WINDOWS):
        cols = slice(g * POOL_GROUP_WIDTH, (g + 1) * POOL_GROUP_WIDTH)
        wsum = u_pool[:, cols]
        for lag in range(1, w):
            wsum = wsum + ext_ref[POOL_HALO - lag:POOL_HALO - lag + BLOCK_T, cols]
        pooled = wsum / jnp.minimum(pos, float(w)) - u_pool[:, cols]
        mixed.append(_dot(pooled.astype(bf16), poolw_ref[g]))
    ext_ref[0:POOL_HALO, :] = ext_ref[BLOCK_T:BLOCK_T + POOL_HALO, :]
    y_pool = jnp.concatenate(mixed, axis=-1) * pscale_ref[...] * _silu(pn[:, D_MODEL:2 * D_MODEL])

    hp = _dot(perm_ref[...], hb).astype(bf16)
    ps = _dot(hp, win_ref[:, 2 * D_MODEL:4 * D_MODEL])
    u_ssm = ps[:, 0:D_MODEL]
    ub = u_ssm.astype(bf16)

    row = lax.broadcasted_iota(jnp.int32, (SUBLANES, kb), 0)
    y_parts = []
    for k in range(N_KBLOCKS):
        bu_ref = bu2_ref.at[k % 2]
        xs_ref = xs2_ref.at[k % 2]
        bu_ref[...] = _dot(ub[:, k * MXU_DIM:(k + 1) * MXU_DIM], bblk_ref[k])
        ar = pow_ref[k, _POW_STEP, 0]
        ai = pow_ref[k, _POW_STEP, 1]

        def scan_step(i, xr, xi):
            rows = slice(i * SUBLANES, (i + 1) * SUBLANES)
            return _cmul_add(ar, ai, xr, xi, bu_ref[rows, 0:kb], bu_ref[rows, kb:2 * kb])

        er = bu_ref[0:SUBLANES, 0:kb]
        ei = bu_ref[0:SUBLANES, kb:2 * kb]
        for i in range(1, SCAN_STEPS):
            er, ei = scan_step(i, er, ei)
        for idx, sh in ((_POW_S1, 1), (_POW_S2, 2), (_POW_S4, 4)):
            sr = jnp.where(row >= sh, pltpu.roll(er, sh, 0), 0.0)
            si = jnp.where(row >= sh, pltpu.roll(ei, sh, 0), 0.0)
            er, ei = _cmul_add(pow_ref[k, idx, 0], pow_ref[k, idx, 1], sr, si, er, ei)
        cr = carry_ref[k, 0]
        ci = carry_ref[k, 1]
        er, ei = _cmul_add(pow_ref[k, _POW_CARRY, 0], pow_ref[k, _POW_CARRY, 1], cr, ci, er, ei)
        carry_ref[k, 0] = jnp.broadcast_to(er[SUBLANES - 1:SUBLANES], (SUBLANES, kb))
        carry_ref[k, 1] = jnp.broadcast_to(ei[SUBLANES - 1:SUBLANES], (SUBLANES, kb))
        x0r = jnp.where(row == 0, cr, pltpu.roll(er, 1, 0))
        x0i = jnp.where(row == 0, ci, pltpu.roll(ei, 1, 0))

        xr, xi = x0r, x0i
        for i2 in range(SCAN_STEPS // 2):
            xr1, xi1 = scan_step(2 * i2, xr, xi)
            xr, xi = scan_step(2 * i2 + 1, xr1, xi1)
            rows = slice(i2 * 2 * SUBLANES, (i2 + 1) * 2 * SUBLANES)
            xs_ref[rows, 0:kb] = jnp.concatenate([xr1, xr], axis=0).astype(bf16)
            xs_ref[rows, kb:2 * kb] = jnp.concatenate([xi1, xi], axis=0).astype(bf16)
        y_parts.append(_dot(xs_ref[...], cblk_ref[k]))

    y = _gelu_tanh(jnp.concatenate(y_parts, axis=-1) + dskip_ref[...] * u_ssm)
    y = y * _sigmoid(_dot(y.astype(bf16), gluw_ref[...]) + glub_ref[...])
    y_ssm = (y * _silu(ps[:, D_MODEL:2 * D_MODEL])).astype(bf16)
    y_ssm = _dot(permt_ref[...], y_ssm).astype(bf16)

    merged = (_sigmoid(gates[:, 0:D_MODEL]) * _dot(y_pool.astype(bf16), wbp_ref[...])
              + _sigmoid(gates[:, D_MODEL:2 * D_MODEL]) * _dot(y_ssm, wbs_ref[...]))
    out = _dot(merged.astype(bf16), wout_ref[...])
    rn = out * lax.rsqrt(jnp.mean(out * out, axis=-1, keepdims=True) + RMS_EPS) * npost_ref[...]
    o_ref[0] = x + gate * rn


def _const_spec(shape):
    nd = len(shape)
    return pl.BlockSpec(shape, lambda b, j: (0,) * nd, pipeline_mode=pl.Buffered(1))


def _scan_permutation():
    p = np.zeros((BLOCK_T, BLOCK_T), np.float32)
    for i in range(SCAN_STEPS):
        for q in range(SUBLANES):
            p[SUBLANES * i + q, q * SCAN_STEPS + i] = 1.0
    return p


def _layer(x, c, w_ada, b_ada, norm_pre, norm_post, w_in, pool_w, pool_scale, a_re, a_im, log_dt,
           b_re, b_im, c_re, c_im, d_skip, glu_w, glu_b, w_branch_pool, w_branch_ssm, w_out):
    bsz, seq, d = x.shape
    assert d == D_MODEL and seq % BLOCK_T == 0
    bf16 = jnp.bfloat16
    f32 = jnp.float32

    bblk, cblk, pow_tab = _ssm_prep(a_re, a_im, log_dt, b_re, b_im, c_re, c_im)
    c_pad = jnp.zeros((SUBLANES, d), f32).at[:bsz].set(c)
    mod = _adaln(c_pad, w_ada, b_ada)[:bsz].reshape(bsz, 3, d)
    perm = _scan_permutation()

    row = lambda v: v.reshape(1, d)
    operands = [
        (x, pl.BlockSpec((1, BLOCK_T, d), lambda b, j: (b, j, 0))),
        (mod, pl.BlockSpec((1, 3, d), lambda b, j: (b, 0, 0))),
        (row(norm_pre), None), (row(norm_post), None),
        (w_in.astype(bf16), None),
        (jnp.asarray(perm, bf16), None), (jnp.asarray(perm.T, bf16), None),
        (pool_w.astype(bf16), None), (row(pool_scale), None),
        (pow_tab, None), (bblk, None), (cblk, None), (row(d_skip), None),
        (glu_w.astype(bf16), None), (row(glu_b), None),
        (w_branch_pool.astype(bf16), None), (w_branch_ssm.astype(bf16), None),
        (w_out.astype(bf16), None),
    ]
    arrays = [a for a, _ in operands]
    specs = [s if s is not None else _const_spec(a.shape) for a, s in operands]

    return pl.pallas_call(
        _block_kernel,
        grid=(bsz, seq // BLOCK_T),
        in_specs=specs,
        out_specs=pl.BlockSpec((1, BLOCK_T, d), lambda b, j: (b, j, 0)),
        out_shape=jax.ShapeDtypeStruct((bsz, seq, d), x.dtype),
        scratch_shapes=[
            pltpu.VMEM((BLOCK_T + POOL_HALO, d), f32),
            pltpu.VMEM((2, BLOCK_T, 2 * KBLOCK_STATES), f32),
            pltpu.VMEM((2, BLOCK_T, 2 * KBLOCK_STATES), bf16),
            pltpu.VMEM((N_KBLOCKS, 2, SUBLANES, KBLOCK_STATES), f32),
        ],
        compiler_params=pltpu.CompilerParams(
            dimension_semantics=("arbitrary", "arbitrary"),
            vmem_limit_bytes=VMEM_LIMIT_BYTES),
        name="block",
    )(*arrays)


def kernel(x, c, w_ada, b_ada, norm_pre, norm_post, w_in, pool_w, pool_scale, ssm_a_re, ssm_a_im,
           ssm_log_dt, ssm_b_re, ssm_b_im, ssm_c_re, ssm_c_im, ssm_d, glu_w, glu_b, w_branch_pool,
           w_branch_ssm, w_out):
    for layer in range(w_in.shape[0]):
        x = _layer(x, c, w_ada[layer], b_ada[layer], norm_pre[layer], norm_post[layer],
                   w_in[layer], pool_w[layer], pool_scale[layer], ssm_a_re[layer],
                   ssm_a_im[layer], ssm_log_dt[layer], ssm_b_re[layer], ssm_b_im[layer],
                   ssm_c_re[layer], ssm_c_im[layer], ssm_d.reshape(ssm_d.shape[0], -1)[layer],
                   glu_w[layer], glu_b[layer], w_branch_pool[layer], w_branch_ssm[layer],
                   w_out[layer])
    return x
```

```python
import math

import jax
import jax.numpy as jnp
from jax import lax
from jax.experimental import pallas as pl
from jax.experimental.pallas import tpu as pltpu

D_MODEL = 1024
POOL_WINDOWS = (2, 4, 8, 16)
POOL_GROUP_WIDTH = D_MODEL // len(POOL_WINDOWS)
POOL_HALO = 16
SSM_GROUP = 16
SSM_GROUPS = D_MODEL // SSM_GROUP
SSM_STATE = 64
RMS_EPS = 1e-6

SUBLANES = 8
LANES = 128
MXU_DIM = 256
CHUNK = 16
GROUPS_PER_SLAB = LANES // SSM_GROUP
N_SLABS = D_MODEL // LANES
PAIRS_PER_SLAB = GROUPS_PER_SLAB // 2
PAIR_STATES = 2 * SSM_STATE
SLAB_IN = CHUNK * LANES
SLAB_STATES = PAIRS_PER_SLAB * 2 * PAIR_STATES
BLOCK_T = 256
SSM_BLOCK_T = 4096
SSM_BLOCK_CHUNKS = SSM_BLOCK_T // CHUNK
VMEM_LIMIT_BYTES = 56 * 1024 * 1024

_ROW_A1, _ROW_A2, _ROW_A4, _ROW_CARRY = 0, 1, 2, 8
_POW_ROWS = 16


def _sigmoid(v):
    return 0.5 * jnp.tanh(0.5 * v) + 0.5


def _silu(v):
    return v * _sigmoid(v)


def _gelu_tanh(v):
    c = math.sqrt(2.0 / math.pi)
    return 0.5 * v * (1.0 + jnp.tanh(c * (v + 0.044715 * (v * v * v))))


def _dot(a, b):
    return jnp.dot(a, b, preferred_element_type=jnp.float32)


def _dot_nt(a, b, precision=None):
    return lax.dot_general(a, b, (((1,), (1,)), ((), ())), precision=precision,
                           preferred_element_type=jnp.float32)


def _cmul_add(ar, ai, xr, xi, br, bi):
    return ar * xr - ai * xi + br, ar * xi + ai * xr + bi


def _rms_modulate(x, norm_gain, scale, shift):
    xn = x * lax.rsqrt(jnp.mean(x * x, axis=-1, keepdims=True) + RMS_EPS) * norm_gain
    return xn * (1.0 + scale) + shift


def _const_spec(shape):
    nd = len(shape)
    return pl.BlockSpec(shape, lambda *_: (0,) * nd, pipeline_mode=pl.Buffered(1))


def _ssm_prep_kernel(are_ref, aim_ref, ldt_ref, bre_ref, bim_ref, cre_ref, cim_ref,
                     lag_ref, bst_ref, cstt_ref, pow_ref):
    bf16 = jnp.bfloat16
    in_group = [lax.broadcasted_iota(jnp.int32, (SSM_GROUP, LANES), 1) // SSM_STATE == g
                for g in range(2)]
    lane_block = lax.broadcasted_iota(jnp.int32, (SSM_GROUP, LANES), 1) // SSM_GROUP

    def split(v_re, v_im, g):
        return jnp.concatenate([jnp.where(in_group[g], v_re, 0.0), jnp.where(in_group[g], v_im, 0.0)],
                               axis=1)

    pow_ref[...] = jnp.zeros_like(pow_ref)
    for m in range(PAIRS_PER_SLAB):
        lanes = slice(m * LANES, (m + 1) * LANES)
        dt = jnp.exp(ldt_ref[:, lanes])
        lam_re = jnp.minimum(are_ref[:, lanes], -1e-4)
        lam_im = aim_ref[:, lanes]
        l_re = lam_re * dt
        l_im = lam_im * dt

        def abar_pow(n):
            mag = jnp.exp(float(n) * l_re)
            return mag * jnp.cos(float(n) * l_im), mag * jnp.sin(float(n) * l_im)

        pows = [abar_pow(n) for n in range(CHUNK + 1)]
        abar_re, abar_im = pows[1]
        den = lam_re * lam_re + lam_im * lam_im
        num_re = abar_re - 1.0
        f_re = (num_re * lam_re + abar_im * lam_im) / den
        f_im = (abar_im * lam_re - num_re * lam_im) / den
        b_re = bre_ref[:, lanes]
        b_im = bim_ref[:, lanes]
        bb_re = f_re * b_re - f_im * b_im
        bb_im = f_re * b_im + f_im * b_re
        c_re = cre_ref[:, lanes]
        c_im = cim_ref[:, lanes]

        for s in range(CHUNK):
            ar, ai = pows[CHUNK - 1 - s]
            w_re = ar * bb_re - ai * bb_im
            w_im = ar * bb_im + ai * bb_re
            for g in range(2):
                bst_ref[0, m, s, g] = split(w_re, w_im, g).astype(bf16)

        z = []
        for n in range(CHUNK + 1):
            ar, ai = pows[n]
            z.append((ar * c_re - ai * c_im, -(ar * c_im + ai * c_re)))
        for t in range(CHUNK):
            for g in range(2):
                cstt_ref[0, m, t, g] = split(z[t + 1][0], z[t + 1][1], g).astype(bf16)

        z_all = jnp.concatenate([jnp.concatenate([z[n][0], z[n][1]], axis=1) for n in range(CHUNK)],
                                axis=0)
        for g in range(2):
            gl = 2 * m + g
            lag = _dot_nt(split(bb_re, bb_im, g), z_all, precision=lax.Precision.HIGHEST)
            for n in range(CHUNK):
                src = lag[:, (n // GROUPS_PER_SLAB) * LANES:(n // GROUPS_PER_SLAB + 1) * LANES]
                shift = (SSM_GROUP * (gl - n % GROUPS_PER_SLAB)) % LANES
                moved = pltpu.roll(src, shift, 1) if shift else src
                lag_ref[0, gl, :, n * LANES:(n + 1) * LANES] = jnp.where(lane_block == gl, moved,
                                                                         0.0).astype(bf16)

        def pow_row(n):
            pr, pi = abar_pow(n)
            return jnp.concatenate([pr, pi], axis=1)

        cols = slice(m * 2 * PAIR_STATES, (m + 1) * 2 * PAIR_STATES)
        pow_ref[_ROW_A1:_ROW_A1 + 1, cols] = pow_row(CHUNK)
        pow_ref[_ROW_A2:_ROW_A2 + 1, cols] = pow_row(2 * CHUNK)
        pow_ref[_ROW_A4:_ROW_A4 + 1, cols] = pow_row(4 * CHUNK)
        for r in range(SUBLANES):
            pow_ref[_ROW_CARRY + r:_ROW_CARRY + r + 1, cols] = pow_row(CHUNK * (r + 1))


def _ssm_prep(a_re, a_im, log_dt, b_re, b_im, c_re, c_im):
    g, p, h = SSM_GROUPS, SSM_STATE, SSM_GROUP
    slab_lanes = GROUPS_PER_SLAB * p
    flat = lambda v: v.reshape(1, g * p)
    row_spec = pl.BlockSpec((1, slab_lanes), lambda v: (0, v))
    mat_spec = pl.BlockSpec((h, slab_lanes), lambda v: (0, v))
    bf16 = jnp.bfloat16
    compact = (N_SLABS, PAIRS_PER_SLAB, CHUNK, 2, SSM_GROUP, 2 * PAIR_STATES)
    return pl.pallas_call(
        _ssm_prep_kernel,
        grid=(N_SLABS,),
        in_specs=[row_spec, row_spec, row_spec, mat_spec, mat_spec, mat_spec, mat_spec],
        out_specs=(pl.BlockSpec((1, GROUPS_PER_SLAB, SSM_GROUP, SLAB_IN), lambda v: (v, 0, 0, 0)),
                   pl.BlockSpec((1,) + compact[1:], lambda v: (v, 0, 0, 0, 0, 0)),
                   pl.BlockSpec((1,) + compact[1:], lambda v: (v, 0, 0, 0, 0, 0)),
                   pl.BlockSpec((_POW_ROWS, SLAB_STATES), lambda v: (0, v))),
        out_shape=(jax.ShapeDtypeStruct((N_SLABS, GROUPS_PER_SLAB, SSM_GROUP, SLAB_IN), bf16),
                   jax.ShapeDtypeStruct(compact, bf16),
                   jax.ShapeDtypeStruct(compact, bf16),
                   jax.ShapeDtypeStruct((_POW_ROWS, N_SLABS * SLAB_STATES), jnp.float32)),
        name="ssm_prep",
    )(flat(a_re), flat(a_im), flat(jnp.repeat(log_dt, p)),
      b_re.reshape(g * p, h).T, b_im.reshape(g * p, h).T,
      c_re.transpose(1, 0, 2).reshape(h, g * p), c_im.transpose(1, 0, 2).reshape(h, g * p))


def _adaln_kernel(c_ref, w_ref, b_ref, o_ref):
    o_ref[...] = _dot(_silu(c_ref[...]), w_ref[...]) + b_ref[...]


def _adaln(c_pad, w_ada, b_ada):
    rows = c_pad.shape[0]
    return pl.pallas_call(
        _adaln_kernel,
        grid=(3,),
        in_specs=[pl.BlockSpec((rows, D_MODEL), lambda j: (0, 0)),
                  pl.BlockSpec((D_MODEL, D_MODEL), lambda j: (0, j)),
                  pl.BlockSpec((1, D_MODEL), lambda j: (0, j))],
        out_specs=pl.BlockSpec((rows, D_MODEL), lambda j: (0, j)),
        out_shape=jax.ShapeDtypeStruct((rows, 3 * D_MODEL), jnp.float32),
        name="adaln",
    )(c_pad, w_ada, b_ada.reshape(1, 3 * D_MODEL))


def _ssm_in_kernel(x_ref, mod_ref, npre_ref, wu_ref, u_ref):
    hb = _rms_modulate(x_ref[0], npre_ref[...], mod_ref[0, 1:2, :], mod_ref[0, 0:1, :])
    u = _dot(hb.astype(jnp.bfloat16), wu_ref[...])
    for v in range(N_SLABS):
        u_ref[0, v] = u[:, v * LANES:(v + 1) * LANES]


def _ssm_in(x, mod, norm_pre, w_u):
    bsz, seq, d = x.shape
    return pl.pallas_call(
        _ssm_in_kernel,
        grid=(bsz, seq // BLOCK_T),
        in_specs=[pl.BlockSpec((1, BLOCK_T, d), lambda b, j: (b, j, 0)),
                  pl.BlockSpec((1, 3, d), lambda b, j: (b, 0, 0)),
                  _const_spec((1, d)), _const_spec((d, d))],
        out_specs=pl.BlockSpec((1, N_SLABS, BLOCK_T, LANES), lambda b, j: (b, 0, j, 0)),
        out_shape=jax.ShapeDtypeStruct((bsz, N_SLABS, seq, LANES), jnp.float32),
        compiler_params=pltpu.CompilerParams(dimension_semantics=("arbitrary", "arbitrary"),
                                             vmem_limit_bytes=VMEM_LIMIT_BYTES),
        name="ssm_in",
    )(x, mod, norm_pre.reshape(1, d), w_u)


def _ssm_core_kernel(u_ref, lag_ref, bstc_ref, csttc_ref, pow_ref, y_ref,
                     toep_ref, bst_ref, cstt_ref, s_ref, xprev_ref, carry_ref):
    first_visit = (pl.program_id(1) == 0) & (pl.program_id(2) == 0)
    bf16 = jnp.bfloat16
    f32 = jnp.float32

    @pl.when(first_visit)
    def _():
        toep_ref[...] = jnp.zeros_like(toep_ref)
        bst_ref[...] = jnp.zeros_like(bst_ref)
        cstt_ref[...] = jnp.zeros_like(cstt_ref)
        for gl in range(GROUPS_PER_SLAB):
            m, g = gl // 2, gl % 2
            cols = slice(m * 2 * PAIR_STATES, (m + 1) * 2 * PAIR_STATES)
            for s in range(CHUNK):
                rows = slice(s * LANES + gl * SSM_GROUP, s * LANES + (gl + 1) * SSM_GROUP)
                toep_ref[rows, s * LANES:] = lag_ref[0, gl, :, 0:SLAB_IN - s * LANES]
                bst_ref[rows, cols] = bstc_ref[0, m, s, g]
                cstt_ref[rows, cols] = csttc_ref[0, m, s, g]

    @pl.when(pl.program_id(2) == 0)
    def _():
        carry_ref[...] = jnp.zeros_like(carry_ref)

    ub = u_ref[0, 0].astype(bf16)
    s_ref[...] = _dot(ub, bst_ref[...])

    row = lax.broadcasted_iota(jnp.int32, (SUBLANES, PAIR_STATES), 0)
    for m in range(PAIRS_PER_SLAB):
        re = slice(m * 2 * PAIR_STATES, m * 2 * PAIR_STATES + PAIR_STATES)
        im = slice(m * 2 * PAIR_STATES + PAIR_STATES, (m + 1) * 2 * PAIR_STATES)
        bcast = lambda r, lanes: jnp.broadcast_to(pow_ref[r:r + 1, lanes], (SUBLANES, PAIR_STATES))
        steps = [(sh, bcast(r, re), bcast(r, im)) for sh, r in ((1, _ROW_A1), (2, _ROW_A2), (4, _ROW_A4))]
        pc_r = pow_ref[_ROW_CARRY:_ROW_CARRY + SUBLANES, re]
        pc_i = pow_ref[_ROW_CARRY:_ROW_CARRY + SUBLANES, im]
        cr = carry_ref[m, 0]
        ci = carry_ref[m, 1]
        for c0 in range(0, SSM_BLOCK_CHUNKS, 2 * SUBLANES):
            prev = []
            for o in range(2):
                rows = slice(c0 + o * SUBLANES, c0 + (o + 1) * SUBLANES)
                er = s_ref[rows, re]
                ei = s_ref[rows, im]
                for sh, ar, ai in steps:
                    tr = jnp.where(row >= sh, pltpu.roll(er, sh, 0), 0.0)
                    ti = jnp.where(row >= sh, pltpu.roll(ei, sh, 0), 0.0)
                    er, ei = _cmul_add(ar, ai, tr, ti, er, ei)
                er, ei = _cmul_add(pc_r, pc_i, cr, ci, er, ei)
                prev.append((jnp.where(row == 0, cr, pltpu.roll(er, 1, 0)),
                             jnp.where(row == 0, ci, pltpu.roll(ei, 1, 0))))
                cr = jnp.broadcast_to(er[SUBLANES - 1:SUBLANES], (SUBLANES, PAIR_STATES))
                ci = jnp.broadcast_to(ei[SUBLANES - 1:SUBLANES], (SUBLANES, PAIR_STATES))
            rows = slice(c0, c0 + 2 * SUBLANES)
            xprev_ref[rows, re] = jnp.concatenate([prev[0][0], prev[1][0]], axis=0).astype(bf16)
            xprev_ref[rows, im] = jnp.concatenate([prev[0][1], prev[1][1]], axis=0).astype(bf16)
        carry_ref[m, 0] = cr
        carry_ref[m, 1] = ci

    xp = xprev_ref[...]
    for nt in range(SLAB_IN // MXU_DIM):
        cols = slice(nt * MXU_DIM, (nt + 1) * MXU_DIM)
        k_hi = (nt + 1) * MXU_DIM
        y_ref[0, 0, :, cols] = (_dot(ub[:, 0:k_hi], toep_ref[0:k_hi, cols])
                                + _dot_nt(xp, cstt_ref[cols, :]))


def _ssm_core(u_slabs, lag, bstc, csttc, pow_tab):
    bsz, n_slabs, seq, lanes = u_slabs.shape
    assert seq % SSM_BLOCK_T == 0 and lanes == LANES
    bf16 = jnp.bfloat16
    f32 = jnp.float32
    u_rows = u_slabs.reshape(bsz, n_slabs, seq // CHUNK, SLAB_IN)
    tok_spec = pl.BlockSpec((1, 1, SSM_BLOCK_CHUNKS, SLAB_IN), lambda v, b, j: (b, v, j, 0))
    compact = bstc.shape
    y_rows = pl.pallas_call(
        _ssm_core_kernel,
        grid=(N_SLABS, bsz, seq // SSM_BLOCK_T),
        in_specs=[tok_spec,
                  pl.BlockSpec((1,) + lag.shape[1:], lambda v, b, j: (v, 0, 0, 0)),
                  pl.BlockSpec((1,) + compact[1:], lambda v, b, j: (v, 0, 0, 0, 0, 0)),
                  pl.BlockSpec((1,) + compact[1:], lambda v, b, j: (v, 0, 0, 0, 0, 0)),
                  pl.BlockSpec((_POW_ROWS, SLAB_STATES), lambda v, b, j: (0, v))],
        out_specs=tok_spec,
        out_shape=jax.ShapeDtypeStruct(u_rows.shape, f32),
        scratch_shapes=[
            pltpu.VMEM((SLAB_IN, SLAB_IN), bf16),
            pltpu.VMEM((SLAB_IN, SLAB_STATES), bf16),
            pltpu.VMEM((SLAB_IN, SLAB_STATES), bf16),
            pltpu.VMEM((SSM_BLOCK_CHUNKS, SLAB_STATES), f32),
            pltpu.VMEM((SSM_BLOCK_CHUNKS, SLAB_STATES), bf16),
            pltpu.VMEM((PAIRS_PER_SLAB, 2, SUBLANES, PAIR_STATES), f32),
        ],
        compiler_params=pltpu.CompilerParams(
            dimension_semantics=("arbitrary", "arbitrary", "arbitrary"),
            vmem_limit_bytes=VMEM_LIMIT_BYTES),
        name="ssm_core",
    )(u_rows, lag, bstc, csttc, pow_tab)
    return y_rows.reshape(u_slabs.shape)


def _block_kernel(x_ref, mod_ref, npre_ref, npost_ref, win_ref, u_ref, ys_ref, poolw_ref, pscale_ref,
                  dskip_ref, gluw_ref, glub_ref, wbp_ref, wbs_ref, wout_ref, o_ref, ext_ref):
    j = pl.program_id(1)
    bf16 = jnp.bfloat16

    @pl.when(j == 0)
    def _():
        ext_ref[0:POOL_HALO, :] = jnp.zeros((POOL_HALO, D_MODEL), jnp.float32)

    x = x_ref[0]
    gate = mod_ref[0, 2:3, :]
    hb = _rms_modulate(x, npre_ref[...], mod_ref[0, 1:2, :], mod_ref[0, 0:1, :]).astype(bf16)

    pn = _dot(hb, win_ref[:, 0:2 * D_MODEL])
    z_ssm = _dot(hb, win_ref[:, 3 * D_MODEL:4 * D_MODEL])
    gates = _dot(hb, win_ref[:, 4 * D_MODEL:6 * D_MODEL])
    u_pool = pn[:, 0:D_MODEL]

    ext_ref[POOL_HALO:, :] = u_pool
    pos = (j * BLOCK_T + 1 + lax.broadcasted_iota(jnp.int32, (BLOCK_T, 1), 0)).astype(jnp.float32)
    mixed = []
    for g, w in enumerate(POOL_WINDOWS):
        cols = slice(g * POOL_GROUP_WIDTH, (g + 1) * POOL_GROUP_WIDTH)
        wsum = u_pool[:, cols]
        for lag in range(1, w):
            wsum = wsum + ext_ref[POOL_HALO - lag:POOL_HALO - lag + BLOCK_T, cols]
        pooled = wsum / jnp.minimum(pos, float(w)) - u_pool[:, cols]
        mixed.append(_dot(pooled.astype(bf16), poolw_ref[g]))
    ext_ref[0:POOL_HALO, :] = ext_ref[BLOCK_T:BLOCK_T + POOL_HALO, :]
    y_pool = jnp.concatenate(mixed, axis=-1) * pscale_ref[...] * _silu(pn[:, D_MODEL:2 * D_MODEL])

    u_ssm = jnp.concatenate([u_ref[0, v] for v in range(N_SLABS)], axis=-1)
    y = jnp.concatenate([ys_ref[0, v] for v in range(N_SLABS)], axis=-1)
    y = _gelu_tanh(y + dskip_ref[...] * u_ssm)
    y = y * _sigmoid(_dot(y.astype(bf16), gluw_ref[...]) + glub_ref[...])
    y_ssm = (y * _silu(z_ssm)).astype(bf16)

    merged = (_sigmoid(gates[:, 0:D_MODEL]) * _dot(y_pool.astype(bf16), wbp_ref[...])
              + _sigmoid(gates[:, D_MODEL:2 * D_MODEL]) * _dot(y_ssm, wbs_ref[...]))
    out = _dot(merged.astype(bf16), wout_ref[...])
    rn = out * lax.rsqrt(jnp.mean(out * out, axis=-1, keepdims=True) + RMS_EPS) * npost_ref[...]
    o_ref[0] = x + gate * rn


def _layer(x, c, w_ada, b_ada, norm_pre, norm_post, w_in, pool_w, pool_scale, a_re, a_im, log_dt,
           b_re, b_im, c_re, c_im, d_skip, glu_w, glu_b, w_branch_pool, w_branch_ssm, w_out):
    bsz, seq, d = x.shape
    assert d == D_MODEL and seq % BLOCK_T == 0
    bf16 = jnp.bfloat16
    f32 = jnp.float32

    lag, bstc, csttc, pow_tab = _ssm_prep(a_re, a_im, log_dt, b_re, b_im, c_re, c_im)
    c_pad = jnp.zeros((SUBLANES, d), f32).at[:bsz].set(c)
    mod = _adaln(c_pad, w_ada, b_ada)[:bsz].reshape(bsz, 3, d)
    w_in_b = w_in.astype(bf16)

    u_slabs = _ssm_in(x, mod, norm_pre, w_in_b[:, 2 * d:3 * d])
    y_slabs = _ssm_core(u_slabs, lag, bstc, csttc, pow_tab)

    row = lambda v: v.reshape(1, d)
    tok_spec = pl.BlockSpec((1, BLOCK_T, d), lambda b, j: (b, j, 0))
    slab_spec = pl.BlockSpec((1, N_SLABS, BLOCK_T, LANES), lambda b, j: (b, 0, j, 0))
    operands = [
        (x, tok_spec),
        (mod, pl.BlockSpec((1, 3, d), lambda b, j: (b, 0, 0))),
        (row(norm_pre), None), (row(norm_post), None),
        (w_in_b, None),
        (u_slabs, slab_spec), (y_slabs, slab_spec),
        (pool_w.astype(bf16), None), (row(pool_scale), None), (row(d_skip), None),
        (glu_w.astype(bf16), None), (row(glu_b), None),
        (w_branch_pool.astype(bf16), None), (w_branch_ssm.astype(bf16), None),
        (w_out.astype(bf16), None),
    ]
    arrays = [a for a, _ in operands]
    specs = [s if s is not None else _const_spec(a.shape) for a, s in operands]

    return pl.pallas_call(
        _block_kernel,
        grid=(bsz, seq // BLOCK_T),
        in_specs=specs,
        out_specs=tok_spec,
        out_shape=jax.ShapeDtypeStruct((bsz, seq, d), x.dtype),
        scratch_shapes=[pltpu.VMEM((BLOCK_T + POOL_HALO, d), f32)],
        compiler_params=pltpu.CompilerParams(
            dimension_semantics=("arbitrary", "arbitrary"),
            vmem_limit_bytes=VMEM_LIMIT_BYTES),
        name="block",
    )(*arrays)


def kernel(x, c, w_ada, b_ada, norm_pre, norm_post, w_in, pool_w, pool_scale, ssm_a_re, ssm_a_im,
           ssm_log_dt, ssm_b_re, ssm_b_im, ssm_c_re, ssm_c_im, ssm_d, glu_w, glu_b, w_branch_pool,
           w_branch_ssm, w_out):
    for layer in range(w_in.shape[0]):
        x = _layer(x, c, w_ada[layer], b_ada[layer], norm_pre[layer], norm_post[layer],
                   w_in[layer], pool_w[layer], pool_scale[layer], ssm_a_re[layer],
                   ssm_a_im[layer], ssm_log_dt[layer], ssm_b_re[layer], ssm_b_im[layer],
                   ssm_c_re[layer], ssm_c_im[layer], ssm_d.reshape(ssm_d.shape[0], -1)[layer],
                   glu_w[layer], glu_b[layer], w_branch_pool[layer], w_branch_ssm[layer],
                   w_out[layer])
    return x
```

```python
import math

import jax
import jax.numpy as jnp
from jax import lax
from jax.experimental import pallas as pl
from jax.experimental.pallas import tpu as pltpu

D_MODEL = 1024
POOL_WINDOWS = (2, 4, 8, 16)
POOL_GROUP_WIDTH = D_MODEL // len(POOL_WINDOWS)
POOL_HALO = 16
SSM_GROUP = 16
SSM_GROUPS = D_MODEL // SSM_GROUP
SSM_STATE = 64
RMS_EPS = 1e-6

SUBLANES = 8
LANES = 128
MXU_DIM = 256
CHUNK = 16
GROUPS_PER_SLAB = LANES // SSM_GROUP
N_SLABS = D_MODEL // LANES
PAIRS_PER_SLAB = GROUPS_PER_SLAB // 2
PAIR_STATES = 2 * SSM_STATE
SLAB_IN = CHUNK * LANES
SLAB_STATES = PAIRS_PER_SLAB * 2 * PAIR_STATES
BLOCK_T = 256
SSM_IN_BLOCK_T = 1024
SSM_BLOCK_T = 4096
SSM_BLOCK_CHUNKS = SSM_BLOCK_T // CHUNK
VMEM_LIMIT_BYTES = 56 * 1024 * 1024

_ROW_A1, _ROW_A2, _ROW_A4, _ROW_CARRY = 0, 1, 2, 8
_POW_ROWS = 16


def _sigmoid(v):
    return 0.5 * jnp.tanh(0.5 * v) + 0.5


def _silu(v):
    return v * _sigmoid(v)


def _gelu_tanh(v):
    c = math.sqrt(2.0 / math.pi)
    return 0.5 * v * (1.0 + jnp.tanh(c * (v + 0.044715 * (v * v * v))))


def _dot(a, b):
    return jnp.dot(a, b, preferred_element_type=jnp.float32)


def _dot_nt(a, b, precision=None):
    return lax.dot_general(a, b, (((1,), (1,)), ((), ())), precision=precision,
                           preferred_element_type=jnp.float32)


def _cmul_add(ar, ai, xr, xi, br, bi):
    return ar * xr - ai * xi + br, ar * xi + ai * xr + bi


def _rms_modulate(x, norm_gain, scale, shift):
    xn = x * lax.rsqrt(jnp.mean(x * x, axis=-1, keepdims=True) + RMS_EPS) * norm_gain
    return xn * (1.0 + scale) + shift


def _const_spec(shape):
    nd = len(shape)
    return pl.BlockSpec(shape, lambda *_: (0,) * nd, pipeline_mode=pl.Buffered(1))


def _ssm_prep_kernel(are_ref, aim_ref, ldt_ref, bre_ref, bim_ref, cre_ref, cim_ref,
                     lag_ref, bst_ref, cstt_ref, pow_ref):
    bf16 = jnp.bfloat16
    in_group = [lax.broadcasted_iota(jnp.int32, (SSM_GROUP, LANES), 1) // SSM_STATE == g
                for g in range(2)]
    lane_block = lax.broadcasted_iota(jnp.int32, (SSM_GROUP, LANES), 1) // SSM_GROUP

    def split(v_re, v_im, g):
        return jnp.concatenate([jnp.where(in_group[g], v_re, 0.0), jnp.where(in_group[g], v_im, 0.0)],
                               axis=1)

    pow_ref[...] = jnp.zeros_like(pow_ref)
    for m in range(PAIRS_PER_SLAB):
        lanes = slice(m * LANES, (m + 1) * LANES)
        dt = jnp.exp(ldt_ref[:, lanes])
        lam_re = jnp.minimum(are_ref[:, lanes], -1e-4)
        lam_im = aim_ref[:, lanes]
        l_re = lam_re * dt
        l_im = lam_im * dt

        def abar_pow(n):
            mag = jnp.exp(float(n) * l_re)
            return mag * jnp.cos(float(n) * l_im), mag * jnp.sin(float(n) * l_im)

        pows = [abar_pow(n) for n in range(CHUNK + 1)]
        abar_re, abar_im = pows[1]
        den = lam_re * lam_re + lam_im * lam_im
        num_re = abar_re - 1.0
        f_re = (num_re * lam_re + abar_im * lam_im) / den
        f_im = (abar_im * lam_re - num_re * lam_im) / den
        b_re = bre_ref[:, lanes]
        b_im = bim_ref[:, lanes]
        bb_re = f_re * b_re - f_im * b_im
        bb_im = f_re * b_im + f_im * b_re
        c_re = cre_ref[:, lanes]
        c_im = cim_ref[:, lanes]

        for s in range(CHUNK):
            ar, ai = pows[CHUNK - 1 - s]
            w_re = ar * bb_re - ai * bb_im
            w_im = ar * bb_im + ai * bb_re
            for g in range(2):
                bst_ref[0, m, s, g] = split(w_re, w_im, g).astype(bf16)

        z = []
        for n in range(CHUNK + 1):
            ar, ai = pows[n]
            z.append((ar * c_re - ai * c_im, -(ar * c_im + ai * c_re)))
        for t in range(CHUNK):
            for g in range(2):
                cstt_ref[0, m, t, g] = split(z[t + 1][0], z[t + 1][1], g).astype(bf16)

        z_all = jnp.concatenate([jnp.concatenate([z[n][0], z[n][1]], axis=1) for n in range(CHUNK)],
                                axis=0)
        for g in range(2):
            gl = 2 * m + g
            lag = _dot_nt(split(bb_re, bb_im, g), z_all, precision=lax.Precision.HIGHEST)
            for n in range(CHUNK):
                src = lag[:, (n // GROUPS_PER_SLAB) * LANES:(n // GROUPS_PER_SLAB + 1) * LANES]
                shift = (SSM_GROUP * (gl - n % GROUPS_PER_SLAB)) % LANES
                moved = pltpu.roll(src, shift, 1) if shift else src
                lag_ref[0, gl, :, n * LANES:(n + 1) * LANES] = jnp.where(lane_block == gl, moved,
                                                                         0.0).astype(bf16)

        def pow_row(n):
            pr, pi = abar_pow(n)
            return jnp.concatenate([pr, pi], axis=1)

        cols = slice(m * 2 * PAIR_STATES, (m + 1) * 2 * PAIR_STATES)
        pow_ref[_ROW_A1:_ROW_A1 + 1, cols] = pow_row(CHUNK)
        pow_ref[_ROW_A2:_ROW_A2 + 1, cols] = pow_row(2 * CHUNK)
        pow_ref[_ROW_A4:_ROW_A4 + 1, cols] = pow_row(4 * CHUNK)
        for r in range(SUBLANES):
            pow_ref[_ROW_CARRY + r:_ROW_CARRY + r + 1, cols] = pow_row(CHUNK * (r + 1))


def _ssm_prep(a_re, a_im, log_dt, b_re, b_im, c_re, c_im):
    g, p, h = SSM_GROUPS, SSM_STATE, SSM_GROUP
    slab_lanes = GROUPS_PER_SLAB * p
    flat = lambda v: v.reshape(1, g * p)
    row_spec = pl.BlockSpec((1, slab_lanes), lambda v: (0, v))
    mat_spec = pl.BlockSpec((h, slab_lanes), lambda v: (0, v))
    bf16 = jnp.bfloat16
    compact = (N_SLABS, PAIRS_PER_SLAB, CHUNK, 2, SSM_GROUP, 2 * PAIR_STATES)
    return pl.pallas_call(
        _ssm_prep_kernel,
        grid=(N_SLABS,),
        in_specs=[row_spec, row_spec, row_spec, mat_spec, mat_spec, mat_spec, mat_spec],
        out_specs=(pl.BlockSpec((1, GROUPS_PER_SLAB, SSM_GROUP, SLAB_IN), lambda v: (v, 0, 0, 0)),
                   pl.BlockSpec((1,) + compact[1:], lambda v: (v, 0, 0, 0, 0, 0)),
                   pl.BlockSpec((1,) + compact[1:], lambda v: (v, 0, 0, 0, 0, 0)),
                   pl.BlockSpec((_POW_ROWS, SLAB_STATES), lambda v: (0, v))),
        out_shape=(jax.ShapeDtypeStruct((N_SLABS, GROUPS_PER_SLAB, SSM_GROUP, SLAB_IN), bf16),
                   jax.ShapeDtypeStruct(compact, bf16),
                   jax.ShapeDtypeStruct(compact, bf16),
                   jax.ShapeDtypeStruct((_POW_ROWS, N_SLABS * SLAB_STATES), jnp.float32)),
        name="ssm_prep",
    )(flat(a_re), flat(a_im), flat(jnp.repeat(log_dt, p)),
      b_re.reshape(g * p, h).T, b_im.reshape(g * p, h).T,
      c_re.transpose(1, 0, 2).reshape(h, g * p), c_im.transpose(1, 0, 2).reshape(h, g * p))


def _adaln_kernel(c_ref, w_ref, b_ref, o_ref):
    o_ref[...] = _dot(_silu(c_ref[...]), w_ref[...]) + b_ref[...]


def _adaln(c_pad, w_ada, b_ada):
    rows = c_pad.shape[0]
    return pl.pallas_call(
        _adaln_kernel,
        grid=(3,),
        in_specs=[pl.BlockSpec((rows, D_MODEL), lambda j: (0, 0)),
                  pl.BlockSpec((D_MODEL, D_MODEL), lambda j: (0, j)),
                  pl.BlockSpec((1, D_MODEL), lambda j: (0, j))],
        out_specs=pl.BlockSpec((rows, D_MODEL), lambda j: (0, j)),
        out_shape=jax.ShapeDtypeStruct((rows, 3 * D_MODEL), jnp.float32),
        name="adaln",
    )(c_pad, w_ada, b_ada.reshape(1, 3 * D_MODEL))


def _tokens_to_chunk_rows(tok_ref, rows_ref):
    n_chunks = tok_ref.shape[1] // CHUNK
    for v in range(N_SLABS):
        for t in range(CHUNK):
            rows_ref[0, v, :, t * LANES:(t + 1) * LANES] = tok_ref[v, pl.ds(t, n_chunks, stride=CHUNK), :]


def _chunk_rows_to_tokens(rows_ref, tok_ref):
    n_chunks = tok_ref.shape[1] // CHUNK
    for v in range(N_SLABS):
        for t in range(CHUNK):
            tok_ref[v, pl.ds(t, n_chunks, stride=CHUNK), :] = rows_ref[0, v, :, t * LANES:(t + 1) * LANES]


def _ssm_in_kernel(x_ref, mod_ref, npre_ref, wu_ref, u_ref, tok_ref):
    hb = _rms_modulate(x_ref[0], npre_ref[...], mod_ref[0, 1:2, :], mod_ref[0, 0:1, :])
    u = _dot(hb.astype(jnp.bfloat16), wu_ref[...])
    for v in range(N_SLABS):
        tok_ref[v] = u[:, v * LANES:(v + 1) * LANES]
    _tokens_to_chunk_rows(tok_ref, u_ref)


def _ssm_in(x, mod, norm_pre, w_u):
    bsz, seq, d = x.shape
    return pl.pallas_call(
        _ssm_in_kernel,
        grid=(bsz, seq // SSM_IN_BLOCK_T),
        in_specs=[pl.BlockSpec((1, SSM_IN_BLOCK_T, d), lambda b, j: (b, j, 0)),
                  pl.BlockSpec((1, 3, d), lambda b, j: (b, 0, 0)),
                  _const_spec((1, d)), _const_spec((d, d))],
        out_specs=pl.BlockSpec((1, N_SLABS, SSM_IN_BLOCK_T // CHUNK, SLAB_IN), lambda b, j: (b, 0, j, 0)),
        out_shape=jax.ShapeDtypeStruct((bsz, N_SLABS, seq // CHUNK, SLAB_IN), jnp.float32),
        scratch_shapes=[pltpu.VMEM((N_SLABS, SSM_IN_BLOCK_T, LANES), jnp.float32)],
        compiler_params=pltpu.CompilerParams(dimension_semantics=("arbitrary", "arbitrary"),
                                             vmem_limit_bytes=VMEM_LIMIT_BYTES),
        name="ssm_in",
    )(x, mod, norm_pre.reshape(1, d), w_u)


def _ssm_core_kernel(u_ref, lag_ref, bstc_ref, csttc_ref, pow_ref, y_ref,
                     toep_ref, bst_ref, cstt_ref, s_ref, xprev_ref, carry_ref):
    first_visit = (pl.program_id(1) == 0) & (pl.program_id(2) == 0)
    bf16 = jnp.bfloat16
    f32 = jnp.float32

    @pl.when(first_visit)
    def _():
        toep_ref[...] = jnp.zeros_like(toep_ref)
        bst_ref[...] = jnp.zeros_like(bst_ref)
        cstt_ref[...] = jnp.zeros_like(cstt_ref)
        for gl in range(GROUPS_PER_SLAB):
            m, g = gl // 2, gl % 2
            cols = slice(m * 2 * PAIR_STATES, (m + 1) * 2 * PAIR_STATES)
            for s in range(CHUNK):
                rows = slice(s * LANES + gl * SSM_GROUP, s * LANES + (gl + 1) * SSM_GROUP)
                toep_ref[rows, s * LANES:] = lag_ref[0, gl, :, 0:SLAB_IN - s * LANES]
                bst_ref[rows, cols] = bstc_ref[0, m, s, g]
                cstt_ref[rows, cols] = csttc_ref[0, m, s, g]

    @pl.when(pl.program_id(2) == 0)
    def _():
        carry_ref[...] = jnp.zeros_like(carry_ref)

    ub = u_ref[0, 0].astype(bf16)
    s_ref[...] = _dot(ub, bst_ref[...])

    row = lax.broadcasted_iota(jnp.int32, (SUBLANES, PAIR_STATES), 0)
    for m in range(PAIRS_PER_SLAB):
        re = slice(m * 2 * PAIR_STATES, m * 2 * PAIR_STATES + PAIR_STATES)
        im = slice(m * 2 * PAIR_STATES + PAIR_STATES, (m + 1) * 2 * PAIR_STATES)
        bcast = lambda r, lanes: jnp.broadcast_to(pow_ref[r:r + 1, lanes], (SUBLANES, PAIR_STATES))
        steps = [(sh, bcast(r, re), bcast(r, im)) for sh, r in ((1, _ROW_A1), (2, _ROW_A2), (4, _ROW_A4))]
        pc_r = pow_ref[_ROW_CARRY:_ROW_CARRY + SUBLANES, re]
        pc_i = pow_ref[_ROW_CARRY:_ROW_CARRY + SUBLANES, im]
        cr = carry_ref[m, 0]
        ci = carry_ref[m, 1]
        for c0 in range(0, SSM_BLOCK_CHUNKS, 2 * SUBLANES):
            prev = []
            for o in range(2):
                rows = slice(c0 + o * SUBLANES, c0 + (o + 1) * SUBLANES)
                er = s_ref[rows, re]
                ei = s_ref[rows, im]
                for sh, ar, ai in steps:
                    tr = jnp.where(row >= sh, pltpu.roll(er, sh, 0), 0.0)
                    ti = jnp.where(row >= sh, pltpu.roll(ei, sh, 0), 0.0)
                    er, ei = _cmul_add(ar, ai, tr, ti, er, ei)
                er, ei = _cmul_add(pc_r, pc_i, cr, ci, er, ei)
                prev.append((jnp.where(row == 0, cr, pltpu.roll(er, 1, 0)),
                             jnp.where(row == 0, ci, pltpu.roll(ei, 1, 0))))
                cr = jnp.broadcast_to(er[SUBLANES - 1:SUBLANES], (SUBLANES, PAIR_STATES))
                ci = jnp.broadcast_to(ei[SUBLANES - 1:SUBLANES], (SUBLANES, PAIR_STATES))
            rows = slice(c0, c0 + 2 * SUBLANES)
            xprev_ref[rows, re] = jnp.concatenate([prev[0][0], prev[1][0]], axis=0).astype(bf16)
            xprev_ref[rows, im] = jnp.concatenate([prev[0][1], prev[1][1]], axis=0).astype(bf16)
        carry_ref[m, 0] = cr
        carry_ref[m, 1] = ci

    xp = xprev_ref[...]
    for nt in range(SLAB_IN // MXU_DIM):
        cols = slice(nt * MXU_DIM, (nt + 1) * MXU_DIM)
        k_hi = (nt + 1) * MXU_DIM
        y_ref[0, 0, :, cols] = (_dot(ub[:, 0:k_hi], toep_ref[0:k_hi, cols])
                                + _dot_nt(xp, cstt_ref[cols, :]))


def _ssm_core(u_rows, lag, bstc, csttc, pow_tab):
    bsz, n_slabs, n_chunks, lanes = u_rows.shape
    seq = n_chunks * CHUNK
    assert seq % SSM_BLOCK_T == 0 and lanes == SLAB_IN
    bf16 = jnp.bfloat16
    f32 = jnp.float32
    tok_spec = pl.BlockSpec((1, 1, SSM_BLOCK_CHUNKS, SLAB_IN), lambda v, b, j: (b, v, j, 0))
    compact = bstc.shape
    return pl.pallas_call(
        _ssm_core_kernel,
        grid=(N_SLABS, bsz, seq // SSM_BLOCK_T),
        in_specs=[tok_spec,
                  pl.BlockSpec((1,) + lag.shape[1:], lambda v, b, j: (v, 0, 0, 0)),
                  pl.BlockSpec((1,) + compact[1:], lambda v, b, j: (v, 0, 0, 0, 0, 0)),
                  pl.BlockSpec((1,) + compact[1:], lambda v, b, j: (v, 0, 0, 0, 0, 0)),
                  pl.BlockSpec((_POW_ROWS, SLAB_STATES), lambda v, b, j: (0, v))],
        out_specs=tok_spec,
        out_shape=jax.ShapeDtypeStruct(u_rows.shape, f32),
        scratch_shapes=[
            pltpu.VMEM((SLAB_IN, SLAB_IN), bf16),
            pltpu.VMEM((SLAB_IN, SLAB_STATES), bf16),
            pltpu.VMEM((SLAB_IN, SLAB_STATES), bf16),
            pltpu.VMEM((SSM_BLOCK_CHUNKS, SLAB_STATES), f32),
            pltpu.VMEM((SSM_BLOCK_CHUNKS, SLAB_STATES), bf16),
            pltpu.VMEM((PAIRS_PER_SLAB, 2, SUBLANES, PAIR_STATES), f32),
        ],
        compiler_params=pltpu.CompilerParams(
            dimension_semantics=("arbitrary", "arbitrary", "arbitrary"),
            vmem_limit_bytes=VMEM_LIMIT_BYTES),
        name="ssm_core",
    )(u_rows, lag, bstc, csttc, pow_tab)


def _block_kernel(x_ref, mod_ref, npre_ref, npost_ref, win_ref, u_ref, ys_ref, poolw_ref, pscale_ref,
                  dskip_ref, gluw_ref, glub_ref, wbp_ref, wbs_ref, wout_ref, o_ref,
                  ext_ref, utok_ref, ytok_ref):
    j = pl.program_id(1)
    bf16 = jnp.bfloat16

    @pl.when(j == 0)
    def _():
        ext_ref[0:POOL_HALO, :] = jnp.zeros((POOL_HALO, D_MODEL), jnp.float32)

    x = x_ref[0]
    gate = mod_ref[0, 2:3, :]
    hb = _rms_modulate(x, npre_ref[...], mod_ref[0, 1:2, :], mod_ref[0, 0:1, :]).astype(bf16)

    pn = _dot(hb, win_ref[:, 0:2 * D_MODEL])
    z_ssm = _dot(hb, win_ref[:, 3 * D_MODEL:4 * D_MODEL])
    gates = _dot(hb, win_ref[:, 4 * D_MODEL:6 * D_MODEL])
    u_pool = pn[:, 0:D_MODEL]

    ext_ref[POOL_HALO:, :] = u_pool
    pos = (j * BLOCK_T + 1 + lax.broadcasted_iota(jnp.int32, (BLOCK_T, 1), 0)).astype(jnp.float32)
    mixed = []
    for g, w in enumerate(POOL_WINDOWS):
        cols = slice(g * POOL_GROUP_WIDTH, (g + 1) * POOL_GROUP_WIDTH)
        wsum = u_pool[:, cols]
        for lag in range(1, w):
            wsum = wsum + ext_ref[POOL_HALO - lag:POOL_HALO - lag + BLOCK_T, cols]
        pooled = wsum / jnp.minimum(pos, float(w)) - u_pool[:, cols]
        mixed.append(_dot(pooled.astype(bf16), poolw_ref[g]))
    ext_ref[0:POOL_HALO, :] = ext_ref[BLOCK_T:BLOCK_T + POOL_HALO, :]
    y_pool = jnp.concatenate(mixed, axis=-1) * pscale_ref[...] * _silu(pn[:, D_MODEL:2 * D_MODEL])

    _chunk_rows_to_tokens(u_ref, utok_ref)
    _chunk_rows_to_tokens(ys_ref, ytok_ref)
    u_ssm = jnp.concatenate([utok_ref[v] for v in range(N_SLABS)], axis=-1)
    y = jnp.concatenate([ytok_ref[v] for v in range(N_SLABS)], axis=-1)
    y = _gelu_tanh(y + dskip_ref[...] * u_ssm)
    y = y * _sigmoid(_dot(y.astype(bf16), gluw_ref[...]) + glub_ref[...])
    y_ssm = (y * _silu(z_ssm)).astype(bf16)

    merged = (_sigmoid(gates[:, 0:D_MODEL]) * _dot(y_pool.astype(bf16), wbp_ref[...])
              + _sigmoid(gates[:, D_MODEL:2 * D_MODEL]) * _dot(y_ssm, wbs_ref[...]))
    out = _dot(merged.astype(bf16), wout_ref[...])
    rn = out * lax.rsqrt(jnp.mean(out * out, axis=-1, keepdims=True) + RMS_EPS) * npost_ref[...]
    o_ref[0] = x + gate * rn


def _layer(x, c, w_ada, b_ada, norm_pre, norm_post, w_in, pool_w, pool_scale, a_re, a_im, log_dt,
           b_re, b_im, c_re, c_im, d_skip, glu_w, glu_b, w_branch_pool, w_branch_ssm, w_out):
    bsz, seq, d = x.shape
    assert d == D_MODEL and seq % BLOCK_T == 0
    bf16 = jnp.bfloat16
    f32 = jnp.float32

    lag, bstc, csttc, pow_tab = _ssm_prep(a_re, a_im, log_dt, b_re, b_im, c_re, c_im)
    c_pad = jnp.zeros((SUBLANES, d), f32).at[:bsz].set(c)
    mod = _adaln(c_pad, w_ada, b_ada)[:bsz].reshape(bsz, 3, d)
    w_in_b = w_in.astype(bf16)

    u_rows = _ssm_in(x, mod, norm_pre, w_in_b[:, 2 * d:3 * d])
    y_rows = _ssm_core(u_rows, lag, bstc, csttc, pow_tab)

    row = lambda v: v.reshape(1, d)
    tok_spec = pl.BlockSpec((1, BLOCK_T, d), lambda b, j: (b, j, 0))
    rows_spec = pl.BlockSpec((1, N_SLABS, BLOCK_T // CHUNK, SLAB_IN), lambda b, j: (b, 0, j, 0))
    operands = [
        (x, tok_spec),
        (mod, pl.BlockSpec((1, 3, d), lambda b, j: (b, 0, 0))),
        (row(norm_pre), None), (row(norm_post), None),
        (w_in_b, None),
        (u_rows, rows_spec), (y_rows, rows_spec),
        (pool_w.astype(bf16), None), (row(pool_scale), None), (row(d_skip), None),
        (glu_w.astype(bf16), None), (row(glu_b), None),
        (w_branch_pool.astype(bf16), None), (w_branch_ssm.astype(bf16), None),
        (w_out.astype(bf16), None),
    ]
    arrays = [a for a, _ in operands]
    specs = [s if s is not None else _const_spec(a.shape) for a, s in operands]

    return pl.pallas_call(
        _block_kernel,
        grid=(bsz, seq // BLOCK_T),
        in_specs=specs,
        out_specs=tok_spec,
        out_shape=jax.ShapeDtypeStruct((bsz, seq, d), x.dtype),
        scratch_shapes=[pltpu.VMEM((BLOCK_T + POOL_HALO, d), f32),
                        pltpu.VMEM((N_SLABS, BLOCK_T, LANES), f32),
                        pltpu.VMEM((N_SLABS, BLOCK_T, LANES), f32)],
        compiler_params=pltpu.CompilerParams(
            dimension_semantics=("arbitrary", "arbitrary"),
            vmem_limit_bytes=VMEM_LIMIT_BYTES),
        name="block",
    )(*arrays)


def kernel(x, c, w_ada, b_ada, norm_pre, norm_post, w_in, pool_w, pool_scale, ssm_a_re, ssm_a_im,
           ssm_log_dt, ssm_b_re, ssm_b_im, ssm_c_re, ssm_c_im, ssm_d, glu_w, glu_b, w_branch_pool,
           w_branch_ssm, w_out):
    for layer in range(w_in.shape[0]):
        x = _layer(x, c, w_ada[layer], b_ada[layer], norm_pre[layer], norm_post[layer],
                   w_in[layer], pool_w[layer], pool_scale[layer], ssm_a_re[layer],
                   ssm_a_im[layer], ssm_log_dt[layer], ssm_b_re[layer], ssm_b_im[layer],
                   ssm_c_re[layer], ssm_c_im[layer], ssm_d.reshape(ssm_d.shape[0], -1)[layer],
                   glu_w[layer], glu_b[layer], w_branch_pool[layer], w_branch_ssm[layer],
                   w_out[layer])
    return x
```

```python
import math

import jax
import jax.numpy as jnp
from jax import lax
from jax.experimental import pallas as pl
from jax.experimental.pallas import tpu as pltpu

D_MODEL = 1024
POOL_WINDOWS = (2, 4, 8, 16)
POOL_GROUP_WIDTH = D_MODEL // len(POOL_WINDOWS)
POOL_HALO = 16
SSM_GROUP = 16
SSM_GROUPS = D_MODEL // SSM_GROUP
SSM_STATE = 64
RMS_EPS = 1e-6

SUBLANES = 8
LANES = 128
MXU_DIM = 256
CHUNK = 16
GROUPS_PER_SLAB = LANES // SSM_GROUP
N_SLABS = D_MODEL // LANES
PAIRS_PER_SLAB = GROUPS_PER_SLAB // 2
PAIR_STATES = 2 * SSM_STATE
SLAB_IN = CHUNK * LANES
SLAB_STATES = PAIRS_PER_SLAB * 2 * PAIR_STATES
BLOCK_T = 256
SSM_IN_BLOCK_T = 1024
SSM_BLOCK_T = 4096
SSM_BLOCK_CHUNKS = SSM_BLOCK_T // CHUNK
VMEM_LIMIT_BYTES = 56 * 1024 * 1024

_ROW_A1, _ROW_A2, _ROW_A4, _ROW_CARRY = 0, 1, 2, 8
_POW_ROWS = 16


def _sigmoid(v):
    return 0.5 * jnp.tanh(0.5 * v) + 0.5


def _silu(v):
    return v * _sigmoid(v)


def _gelu_tanh(v):
    c = math.sqrt(2.0 / math.pi)
    return 0.5 * v * (1.0 + jnp.tanh(c * (v + 0.044715 * (v * v * v))))


def _dot(a, b):
    return jnp.dot(a, b, preferred_element_type=jnp.float32)


def _dot_nt(a, b, precision=None):
    return lax.dot_general(a, b, (((1,), (1,)), ((), ())), precision=precision,
                           preferred_element_type=jnp.float32)


def _cmul_add(ar, ai, xr, xi, br, bi):
    return ar * xr - ai * xi + br, ar * xi + ai * xr + bi


def _rms_modulate(x, norm_gain, scale, shift):
    xn = x * lax.rsqrt(jnp.mean(x * x, axis=-1, keepdims=True) + RMS_EPS) * norm_gain
    return xn * (1.0 + scale) + shift


def _const_spec(shape):
    nd = len(shape)
    return pl.BlockSpec(shape, lambda *_: (0,) * nd, pipeline_mode=pl.Buffered(1))


def _ssm_prep_kernel(are_ref, aim_ref, ldt_ref, bre_ref, bim_ref, cre_ref, cim_ref,
                     lag_ref, bst_ref, cstt_ref, pow_ref):
    bf16 = jnp.bfloat16
    in_group = [lax.broadcasted_iota(jnp.int32, (SSM_GROUP, LANES), 1) // SSM_STATE == g
                for g in range(2)]
    lane_block = lax.broadcasted_iota(jnp.int32, (SSM_GROUP, LANES), 1) // SSM_GROUP

    def split(v_re, v_im, g):
        return jnp.concatenate([jnp.where(in_group[g], v_re, 0.0), jnp.where(in_group[g], v_im, 0.0)],
                               axis=1)

    pow_ref[...] = jnp.zeros_like(pow_ref)
    for m in range(PAIRS_PER_SLAB):
        lanes = slice(m * LANES, (m + 1) * LANES)
        dt = jnp.exp(ldt_ref[:, lanes])
        lam_re = jnp.minimum(are_ref[:, lanes], -1e-4)
        lam_im = aim_ref[:, lanes]
        l_re = lam_re * dt
        l_im = lam_im * dt

        def cmul(p, q):
            return p[0] * q[0] - p[1] * q[1], p[0] * q[1] + p[1] * q[0]

        mag = jnp.exp(l_re)
        pows = [(jnp.ones_like(l_re), jnp.zeros_like(l_re)), (mag * jnp.cos(l_im), mag * jnp.sin(l_im))]
        for _ in range(CHUNK - 1):
            pows.append(cmul(pows[-1], pows[1]))
        abar_re, abar_im = pows[1]
        den = lam_re * lam_re + lam_im * lam_im
        num_re = abar_re - 1.0
        f_re = (num_re * lam_re + abar_im * lam_im) / den
        f_im = (abar_im * lam_re - num_re * lam_im) / den
        b_re = bre_ref[:, lanes]
        b_im = bim_ref[:, lanes]
        bb_re = f_re * b_re - f_im * b_im
        bb_im = f_re * b_im + f_im * b_re
        c_re = cre_ref[:, lanes]
        c_im = cim_ref[:, lanes]

        for s in range(CHUNK):
            ar, ai = pows[CHUNK - 1 - s]
            w_re = ar * bb_re - ai * bb_im
            w_im = ar * bb_im + ai * bb_re
            for g in range(2):
                bst_ref[0, m, s, g] = split(w_re, w_im, g).astype(bf16)

        z = []
        for n in range(CHUNK + 1):
            ar, ai = pows[n]
            z.append((ar * c_re - ai * c_im, -(ar * c_im + ai * c_re)))
        for t in range(CHUNK):
            for g in range(2):
                cstt_ref[0, m, t, g] = split(z[t + 1][0], z[t + 1][1], g).astype(bf16)

        z_all = jnp.concatenate([jnp.concatenate([z[n][0], z[n][1]], axis=1) for n in range(CHUNK)],
                                axis=0)
        for g in range(2):
            gl = 2 * m + g
            lag = _dot_nt(split(bb_re, bb_im, g), z_all, precision=lax.Precision.HIGHEST)
            for n in range(CHUNK):
                src = lag[:, (n // GROUPS_PER_SLAB) * LANES:(n // GROUPS_PER_SLAB + 1) * LANES]
                shift = (SSM_GROUP * (gl - n % GROUPS_PER_SLAB)) % LANES
                moved = pltpu.roll(src, shift, 1) if shift else src
                lag_ref[0, gl, :, n * LANES:(n + 1) * LANES] = jnp.where(lane_block == gl, moved,
                                                                         0.0).astype(bf16)

        def pow_row(p):
            return jnp.concatenate([p[0], p[1]], axis=1)

        cols = slice(m * 2 * PAIR_STATES, (m + 1) * 2 * PAIR_STATES)
        a16 = pows[CHUNK]
        a32 = cmul(a16, a16)
        pow_ref[_ROW_A1:_ROW_A1 + 1, cols] = pow_row(a16)
        pow_ref[_ROW_A2:_ROW_A2 + 1, cols] = pow_row(a32)
        pow_ref[_ROW_A4:_ROW_A4 + 1, cols] = pow_row(cmul(a32, a32))
        carry_pow = a16
        for r in range(SUBLANES):
            pow_ref[_ROW_CARRY + r:_ROW_CARRY + r + 1, cols] = pow_row(carry_pow)
            carry_pow = cmul(carry_pow, a16)


def _ssm_prep(a_re, a_im, log_dt, b_re, b_im, c_re, c_im):
    g, p, h = SSM_GROUPS, SSM_STATE, SSM_GROUP
    slab_lanes = GROUPS_PER_SLAB * p
    flat = lambda v: v.reshape(1, g * p)
    row_spec = pl.BlockSpec((1, slab_lanes), lambda v: (0, v))
    mat_spec = pl.BlockSpec((h, slab_lanes), lambda v: (0, v))
    bf16 = jnp.bfloat16
    compact = (N_SLABS, PAIRS_PER_SLAB, CHUNK, 2, SSM_GROUP, 2 * PAIR_STATES)
    return pl.pallas_call(
        _ssm_prep_kernel,
        grid=(N_SLABS,),
        in_specs=[row_spec, row_spec, row_spec, mat_spec, mat_spec, mat_spec, mat_spec],
        out_specs=(pl.BlockSpec((1, GROUPS_PER_SLAB, SSM_GROUP, SLAB_IN), lambda v: (v, 0, 0, 0)),
                   pl.BlockSpec((1,) + compact[1:], lambda v: (v, 0, 0, 0, 0, 0)),
                   pl.BlockSpec((1,) + compact[1:], lambda v: (v, 0, 0, 0, 0, 0)),
                   pl.BlockSpec((_POW_ROWS, SLAB_STATES), lambda v: (0, v))),
        out_shape=(jax.ShapeDtypeStruct((N_SLABS, GROUPS_PER_SLAB, SSM_GROUP, SLAB_IN), bf16),
                   jax.ShapeDtypeStruct(compact, bf16),
                   jax.ShapeDtypeStruct(compact, bf16),
                   jax.ShapeDtypeStruct((_POW_ROWS, N_SLABS * SLAB_STATES), jnp.float32)),
        name="ssm_prep",
    )(flat(a_re), flat(a_im), flat(jnp.repeat(log_dt, p)),
      b_re.reshape(g * p, h).T, b_im.reshape(g * p, h).T,
      c_re.transpose(1, 0, 2).reshape(h, g * p), c_im.transpose(1, 0, 2).reshape(h, g * p))


def _adaln_kernel(c_ref, w_ref, b_ref, o_ref):
    o_ref[...] = _dot(_silu(c_ref[...]), w_ref[...]) + b_ref[...]


def _adaln(c_pad, w_ada, b_ada):
    rows = c_pad.shape[0]
    return pl.pallas_call(
        _adaln_kernel,
        grid=(3,),
        in_specs=[pl.BlockSpec((rows, D_MODEL), lambda j: (0, 0)),
                  pl.BlockSpec((D_MODEL, D_MODEL), lambda j: (0, j)),
                  pl.BlockSpec((1, D_MODEL), lambda j: (0, j))],
        out_specs=pl.BlockSpec((rows, D_MODEL), lambda j: (0, j)),
        out_shape=jax.ShapeDtypeStruct((rows, 3 * D_MODEL), jnp.float32),
        name="adaln",
    )(c_pad, w_ada, b_ada.reshape(1, 3 * D_MODEL))


def _tokens_to_chunk_rows(tok_ref, rows_ref):
    n_chunks = tok_ref.shape[1] // CHUNK
    for v in range(N_SLABS):
        for t in range(CHUNK):
            rows_ref[0, v, :, t * LANES:(t + 1) * LANES] = tok_ref[v, pl.ds(t, n_chunks, stride=CHUNK), :]


def _ssm_in_kernel(x_ref, mod_ref, npre_ref, wu_ref, u_ref, tok_ref):
    hb = _rms_modulate(x_ref[0], npre_ref[...], mod_ref[0, 1:2, :], mod_ref[0, 0:1, :])
    u = _dot(hb.astype(jnp.bfloat16), wu_ref[...])
    for v in range(N_SLABS):
        tok_ref[v] = u[:, v * LANES:(v + 1) * LANES]
    _tokens_to_chunk_rows(tok_ref, u_ref)


def _ssm_in(x, mod, norm_pre, w_u):
    bsz, seq, d = x.shape
    return pl.pallas_call(
        _ssm_in_kernel,
        grid=(bsz, seq // SSM_IN_BLOCK_T),
        in_specs=[pl.BlockSpec((1, SSM_IN_BLOCK_T, d), lambda b, j: (b, j, 0)),
                  pl.BlockSpec((1, 3, d), lambda b, j: (b, 0, 0)),
                  _const_spec((1, d)), _const_spec((d, d))],
        out_specs=pl.BlockSpec((1, N_SLABS, SSM_IN_BLOCK_T // CHUNK, SLAB_IN), lambda b, j: (b, 0, j, 0)),
        out_shape=jax.ShapeDtypeStruct((bsz, N_SLABS, seq // CHUNK, SLAB_IN), jnp.float32),
        scratch_shapes=[pltpu.VMEM((N_SLABS, SSM_IN_BLOCK_T, LANES), jnp.float32)],
        compiler_params=pltpu.CompilerParams(dimension_semantics=("arbitrary", "arbitrary"),
                                             vmem_limit_bytes=VMEM_LIMIT_BYTES),
        name="ssm_in",
    )(x, mod, norm_pre.reshape(1, d), w_u)


def _ssm_core_kernel(u_ref, dskip_ref, lag_ref, bstc_ref, csttc_ref, pow_ref, y_ref,
                     toep_ref, bst_ref, cstt_ref, s_ref, xprev_ref, carry_ref):
    first_visit = (pl.program_id(1) == 0) & (pl.program_id(2) == 0)
    bf16 = jnp.bfloat16
    f32 = jnp.float32

    @pl.when(first_visit)
    def _():
        toep_ref[...] = jnp.zeros_like(toep_ref)
        bst_ref[...] = jnp.zeros_like(bst_ref)
        cstt_ref[...] = jnp.zeros_like(cstt_ref)
        for gl in range(GROUPS_PER_SLAB):
            m, g = gl // 2, gl % 2
            cols = slice(m * 2 * PAIR_STATES, (m + 1) * 2 * PAIR_STATES)
            for s in range(CHUNK):
                rows = slice(s * LANES + gl * SSM_GROUP, s * LANES + (gl + 1) * SSM_GROUP)
                toep_ref[rows, s * LANES:] = lag_ref[0, gl, :, 0:SLAB_IN - s * LANES]
                bst_ref[rows, cols] = bstc_ref[0, m, s, g]
                cstt_ref[rows, cols] = csttc_ref[0, m, s, g]

    @pl.when(pl.program_id(2) == 0)
    def _():
        carry_ref[...] = jnp.zeros_like(carry_ref)

    u = u_ref[0, 0]
    ub = u.astype(bf16)
    s_ref[...] = _dot(ub, bst_ref[...])

    y_in = []
    for nt in range(SLAB_IN // MXU_DIM):
        cols = slice(nt * MXU_DIM, (nt + 1) * MXU_DIM)
        k_hi = (nt + 1) * MXU_DIM
        y_in.append(dskip_ref[0, :, cols] * u[:, cols] + _dot(ub[:, 0:k_hi], toep_ref[0:k_hi, cols]))

    row = lax.broadcasted_iota(jnp.int32, (SUBLANES, PAIR_STATES), 0)
    for m in range(PAIRS_PER_SLAB):
        re = slice(m * 2 * PAIR_STATES, m * 2 * PAIR_STATES + PAIR_STATES)
        im = slice(m * 2 * PAIR_STATES + PAIR_STATES, (m + 1) * 2 * PAIR_STATES)
        bcast = lambda r, lanes: jnp.broadcast_to(pow_ref[r:r + 1, lanes], (SUBLANES, PAIR_STATES))
        steps = [(sh, bcast(r, re), bcast(r, im)) for sh, r in ((1, _ROW_A1), (2, _ROW_A2), (4, _ROW_A4))]
        pc_r = pow_ref[_ROW_CARRY:_ROW_CARRY + SUBLANES, re]
        pc_i = pow_ref[_ROW_CARRY:_ROW_CARRY + SUBLANES, im]
        cr = carry_ref[m, 0]
        ci = carry_ref[m, 1]
        for c0 in range(0, SSM_BLOCK_CHUNKS, 2 * SUBLANES):
            prev = []
            for o in range(2):
                rows = slice(c0 + o * SUBLANES, c0 + (o + 1) * SUBLANES)
                er = s_ref[rows, re]
                ei = s_ref[rows, im]
                for sh, ar, ai in steps:
                    tr = jnp.where(row >= sh, pltpu.roll(er, sh, 0), 0.0)
                    ti = jnp.where(row >= sh, pltpu.roll(ei, sh, 0), 0.0)
                    er, ei = _cmul_add(ar, ai, tr, ti, er, ei)
                er, ei = _cmul_add(pc_r, pc_i, cr, ci, er, ei)
                prev.append((jnp.where(row == 0, cr, pltpu.roll(er, 1, 0)),
                             jnp.where(row == 0, ci, pltpu.roll(ei, 1, 0))))
                cr = jnp.broadcast_to(er[SUBLANES - 1:SUBLANES], (SUBLANES, PAIR_STATES))
                ci = jnp.broadcast_to(ei[SUBLANES - 1:SUBLANES], (SUBLANES, PAIR_STATES))
            rows = slice(c0, c0 + 2 * SUBLANES)
            xprev_ref[rows, re] = jnp.concatenate([prev[0][0], prev[1][0]], axis=0).astype(bf16)
            xprev_ref[rows, im] = jnp.concatenate([prev[0][1], prev[1][1]], axis=0).astype(bf16)
        carry_ref[m, 0] = cr
        carry_ref[m, 1] = ci

    xp = xprev_ref[...]
    for nt in range(SLAB_IN // MXU_DIM):
        y_tile = y_in[nt] + _dot_nt(xp, cstt_ref[nt * MXU_DIM:(nt + 1) * MXU_DIM, :])
        for i in range(MXU_DIM // LANES):
            t = nt * (MXU_DIM // LANES) + i
            y_ref[0, 0, pl.ds(t, SSM_BLOCK_CHUNKS, stride=CHUNK), :] = y_tile[:, i * LANES:(i + 1) * LANES]


def _ssm_core(u_rows, d_skip, lag, bstc, csttc, pow_tab):
    bsz, n_slabs, n_chunks, lanes = u_rows.shape
    seq = n_chunks * CHUNK
    assert seq % SSM_BLOCK_T == 0 and lanes == SLAB_IN
    bf16 = jnp.bfloat16
    f32 = jnp.float32
    tok_spec = pl.BlockSpec((1, 1, SSM_BLOCK_CHUNKS, SLAB_IN), lambda v, b, j: (b, v, j, 0))
    d_rows = jnp.tile(d_skip.reshape(n_slabs, 1, LANES), (1, 1, CHUNK))
    compact = bstc.shape
    return pl.pallas_call(
        _ssm_core_kernel,
        grid=(N_SLABS, bsz, seq // SSM_BLOCK_T),
        in_specs=[tok_spec,
                  pl.BlockSpec((1, 1, SLAB_IN), lambda v, b, j: (v, 0, 0)),
                  pl.BlockSpec((1,) + lag.shape[1:], lambda v, b, j: (v, 0, 0, 0)),
                  pl.BlockSpec((1,) + compact[1:], lambda v, b, j: (v, 0, 0, 0, 0, 0)),
                  pl.BlockSpec((1,) + compact[1:], lambda v, b, j: (v, 0, 0, 0, 0, 0)),
                  pl.BlockSpec((_POW_ROWS, SLAB_STATES), lambda v, b, j: (0, v))],
        out_specs=pl.BlockSpec((1, 1, SSM_BLOCK_T, LANES), lambda v, b, j: (b, v, j, 0)),
        out_shape=jax.ShapeDtypeStruct((bsz, n_slabs, seq, LANES), f32),
        scratch_shapes=[
            pltpu.VMEM((SLAB_IN, SLAB_IN), bf16),
            pltpu.VMEM((SLAB_IN, SLAB_STATES), bf16),
            pltpu.VMEM((SLAB_IN, SLAB_STATES), bf16),
            pltpu.VMEM((SSM_BLOCK_CHUNKS, SLAB_STATES), f32),
            pltpu.VMEM((SSM_BLOCK_CHUNKS, SLAB_STATES), bf16),
            pltpu.VMEM((PAIRS_PER_SLAB, 2, SUBLANES, PAIR_STATES), f32),
        ],
        compiler_params=pltpu.CompilerParams(
            dimension_semantics=("arbitrary", "arbitrary", "arbitrary"),
            vmem_limit_bytes=VMEM_LIMIT_BYTES),
        name="ssm_core",
    )(u_rows, d_rows, lag, bstc, csttc, pow_tab)


def _block_kernel(x_ref, mod_ref, npre_ref, npost_ref, win_ref, ys_ref, poolw_ref, pscale_ref,
                  gluw_ref, glub_ref, wbp_ref, wbs_ref, wout_ref, o_ref, ext_ref):
    j = pl.program_id(1)
    bf16 = jnp.bfloat16

    @pl.when(j == 0)
    def _():
        ext_ref[0:POOL_HALO, :] = jnp.zeros((POOL_HALO, D_MODEL), jnp.float32)

    x = x_ref[0]
    gate = mod_ref[0, 2:3, :]
    hb = _rms_modulate(x, npre_ref[...], mod_ref[0, 1:2, :], mod_ref[0, 0:1, :]).astype(bf16)

    pn = _dot(hb, win_ref[:, 0:2 * D_MODEL])
    z_ssm = _dot(hb, win_ref[:, 3 * D_MODEL:4 * D_MODEL])
    gates = _dot(hb, win_ref[:, 4 * D_MODEL:6 * D_MODEL])
    u_pool = pn[:, 0:D_MODEL]

    ext_ref[POOL_HALO:, :] = u_pool
    pos = (j * BLOCK_T + 1 + lax.broadcasted_iota(jnp.int32, (BLOCK_T, 1), 0)).astype(jnp.float32)
    mixed = []
    for g, w in enumerate(POOL_WINDOWS):
        cols = slice(g * POOL_GROUP_WIDTH, (g + 1) * POOL_GROUP_WIDTH)
        wsum = u_pool[:, cols]
        for lag in range(1, w):
            wsum = wsum + ext_ref[POOL_HALO - lag:POOL_HALO - lag + BLOCK_T, cols]
        pooled = wsum / jnp.minimum(pos, float(w)) - u_pool[:, cols]
        mixed.append(_dot(pooled.astype(bf16), poolw_ref[g]))
    ext_ref[0:POOL_HALO, :] = ext_ref[BLOCK_T:BLOCK_T + POOL_HALO, :]
    y_pool = jnp.concatenate(mixed, axis=-1) * pscale_ref[...] * _silu(pn[:, D_MODEL:2 * D_MODEL])

    y = _gelu_tanh(jnp.concatenate([ys_ref[0, v] for v in range(N_SLABS)], axis=-1))
    y = y * _sigmoid(_dot(y.astype(bf16), gluw_ref[...]) + glub_ref[...])
    y_ssm = (y * _silu(z_ssm)).astype(bf16)

    merged = (_sigmoid(gates[:, 0:D_MODEL]) * _dot(y_pool.astype(bf16), wbp_ref[...])
              + _sigmoid(gates[:, D_MODEL:2 * D_MODEL]) * _dot(y_ssm, wbs_ref[...]))
    out = _dot(merged.astype(bf16), wout_ref[...])
    rn = out * lax.rsqrt(jnp.mean(out * out, axis=-1, keepdims=True) + RMS_EPS) * npost_ref[...]
    o_ref[0] = x + gate * rn


def _layer(x, c, w_ada, b_ada, norm_pre, norm_post, w_in, pool_w, pool_scale, a_re, a_im, log_dt,
           b_re, b_im, c_re, c_im, d_skip, glu_w, glu_b, w_branch_pool, w_branch_ssm, w_out):
    bsz, seq, d = x.shape
    assert d == D_MODEL and seq % BLOCK_T == 0
    bf16 = jnp.bfloat16
    f32 = jnp.float32

    lag, bstc, csttc, pow_tab = _ssm_prep(a_re, a_im, log_dt, b_re, b_im, c_re, c_im)
    c_pad = jnp.zeros((SUBLANES, d), f32).at[:bsz].set(c)
    mod = _adaln(c_pad, w_ada, b_ada)[:bsz].reshape(bsz, 3, d)
    w_in_b = w_in.astype(bf16)

    u_rows = _ssm_in(x, mod, norm_pre, w_in_b[:, 2 * d:3 * d])
    y_slabs = _ssm_core(u_rows, d_skip, lag, bstc, csttc, pow_tab)

    row = lambda v: v.reshape(1, d)
    tok_spec = pl.BlockSpec((1, BLOCK_T, d), lambda b, j: (b, j, 0))
    slab_spec = pl.BlockSpec((1, N_SLABS, BLOCK_T, LANES), lambda b, j: (b, 0, j, 0))
    operands = [
        (x, tok_spec),
        (mod, pl.BlockSpec((1, 3, d), lambda b, j: (b, 0, 0))),
        (row(norm_pre), None), (row(norm_post), None),
        (w_in_b, None),
        (y_slabs, slab_spec),
        (pool_w.astype(bf16), None), (row(pool_scale), None),
        (glu_w.astype(bf16), None), (row(glu_b), None),
        (w_branch_pool.astype(bf16), None), (w_branch_ssm.astype(bf16), None),
        (w_out.astype(bf16), None),
    ]
    arrays = [a for a, _ in operands]
    specs = [s if s is not None else _const_spec(a.shape) for a, s in operands]

    return pl.pallas_call(
        _block_kernel,
        grid=(bsz, seq // BLOCK_T),
        in_specs=specs,
        out_specs=tok_spec,
        out_shape=jax.ShapeDtypeStruct((bsz, seq, d), x.dtype),
        scratch_shapes=[pltpu.VMEM((BLOCK_T + POOL_HALO, d), f32)],
        compiler_params=pltpu.CompilerParams(
            dimension_semantics=("arbitrary", "arbitrary"),
            vmem_limit_bytes=VMEM_LIMIT_BYTES),
        name="block",
    )(*arrays)


def kernel(x, c, w_ada, b_ada, norm_pre, norm_post, w_in, pool_w, pool_scale, ssm_a_re, ssm_a_im,
           ssm_log_dt, ssm_b_re, ssm_b_im, ssm_c_re, ssm_c_im, ssm_d, glu_w, glu_b, w_branch_pool,
           w_branch_ssm, w_out):
    for layer in range(w_in.shape[0]):
        x = _layer(x, c, w_ada[layer], b_ada[layer], norm_pre[layer], norm_post[layer],
                   w_in[layer], pool_w[layer], pool_scale[layer], ssm_a_re[layer],
                   ssm_a_im[layer], ssm_log_dt[layer], ssm_b_re[layer], ssm_b_im[layer],
                   ssm_c_re[layer], ssm_c_im[layer], ssm_d.reshape(ssm_d.shape[0], -1)[layer],
                   glu_w[layer], glu_b[layer], w_branch_pool[layer], w_branch_ssm[layer],
                   w_out[layer])
    return x
```

```python
import math

import jax
import jax.numpy as jnp
from jax import lax
from jax.experimental import pallas as pl
from jax.experimental.pallas import tpu as pltpu

D_MODEL = 1024
POOL_WINDOWS = (2, 4, 8, 16)
POOL_GROUP_WIDTH = D_MODEL // len(POOL_WINDOWS)
POOL_HALO = 16
SSM_GROUP = 16
SSM_GROUPS = D_MODEL // SSM_GROUP
SSM_STATE = 64
RMS_EPS = 1e-6

SUBLANES = 8
LANES = 128
MXU_DIM = 256
CHUNK = 16
GROUPS_PER_SLAB = LANES // SSM_GROUP
N_SLABS = D_MODEL // LANES
PAIRS_PER_SLAB = GROUPS_PER_SLAB // 2
PAIR_STATES = 2 * SSM_STATE
SLAB_IN = CHUNK * LANES
SLAB_STATES = PAIRS_PER_SLAB * 2 * PAIR_STATES
BLOCK_T = 512
SUB_T = 256
SSM_IN_BLOCK_T = 1024
SSM_BLOCK_T = 4096
SSM_BLOCK_CHUNKS = SSM_BLOCK_T // CHUNK
VMEM_LIMIT_BYTES = 56 * 1024 * 1024

_ROW_A1, _ROW_A2, _ROW_A4, _ROW_CARRY = 0, 1, 2, 8
_POW_ROWS = 16


def _sigmoid(v):
    return 0.5 * jnp.tanh(0.5 * v) + 0.5


def _silu(v):
    return v * _sigmoid(v)


def _gelu_tanh(v):
    c = math.sqrt(2.0 / math.pi)
    return 0.5 * v * (1.0 + jnp.tanh(c * (v + 0.044715 * (v * v * v))))


def _dot(a, b):
    return jnp.dot(a, b, preferred_element_type=jnp.float32)


def _dot_nt(a, b, precision=None):
    return lax.dot_general(a, b, (((1,), (1,)), ((), ())), precision=precision,
                           preferred_element_type=jnp.float32)


def _cmul_add(ar, ai, xr, xi, br, bi):
    return ar * xr - ai * xi + br, ar * xi + ai * xr + bi


def _rms_modulate(x, norm_gain, scale, shift):
    xn = x * lax.rsqrt(jnp.mean(x * x, axis=-1, keepdims=True) + RMS_EPS) * norm_gain
    return xn * (1.0 + scale) + shift


def _const_spec(shape):
    nd = len(shape)
    return pl.BlockSpec(shape, lambda *_: (0,) * nd, pipeline_mode=pl.Buffered(1))


def _ssm_prep_kernel(are_ref, aim_ref, ldt_ref, bre_ref, bim_ref, cre_ref, cim_ref,
                     lag_ref, bst_ref, cstt_ref, pow_ref):
    bf16 = jnp.bfloat16
    in_group = [lax.broadcasted_iota(jnp.int32, (SSM_GROUP, LANES), 1) // SSM_STATE == g
                for g in range(2)]
    lane_block = lax.broadcasted_iota(jnp.int32, (SSM_GROUP, LANES), 1) // SSM_GROUP

    def split(v_re, v_im, g):
        return jnp.concatenate([jnp.where(in_group[g], v_re, 0.0), jnp.where(in_group[g], v_im, 0.0)],
                               axis=1)

    pow_ref[...] = jnp.zeros_like(pow_ref)
    for m in range(PAIRS_PER_SLAB):
        lanes = slice(m * LANES, (m + 1) * LANES)
        dt = jnp.exp(ldt_ref[:, lanes])
        lam_re = jnp.minimum(are_ref[:, lanes], -1e-4)
        lam_im = aim_ref[:, lanes]
        l_re = lam_re * dt
        l_im = lam_im * dt

        def cmul(p, q):
            return p[0] * q[0] - p[1] * q[1], p[0] * q[1] + p[1] * q[0]

        mag = jnp.exp(l_re)
        pows = [(jnp.ones_like(l_re), jnp.zeros_like(l_re)), (mag * jnp.cos(l_im), mag * jnp.sin(l_im))]
        for _ in range(CHUNK - 1):
            pows.append(cmul(pows[-1], pows[1]))
        abar_re, abar_im = pows[1]
        den = lam_re * lam_re + lam_im * lam_im
        num_re = abar_re - 1.0
        f_re = (num_re * lam_re + abar_im * lam_im) / den
        f_im = (abar_im * lam_re - num_re * lam_im) / den
        b_re = bre_ref[:, lanes]
        b_im = bim_ref[:, lanes]
        bb_re = f_re * b_re - f_im * b_im
        bb_im = f_re * b_im + f_im * b_re
        c_re = cre_ref[:, lanes]
        c_im = cim_ref[:, lanes]

        for s in range(CHUNK):
            ar, ai = pows[CHUNK - 1 - s]
            w_re = ar * bb_re - ai * bb_im
            w_im = ar * bb_im + ai * bb_re
            for g in range(2):
                bst_ref[0, m, s, g] = split(w_re, w_im, g).astype(bf16)

        z = []
        for n in range(CHUNK + 1):
            ar, ai = pows[n]
            z.append((ar * c_re - ai * c_im, -(ar * c_im + ai * c_re)))
        for t in range(CHUNK):
            for g in range(2):
                cstt_ref[0, m, t, g] = split(z[t + 1][0], z[t + 1][1], g).astype(bf16)

        z_all = jnp.concatenate([jnp.concatenate([z[n][0], z[n][1]], axis=1) for n in range(CHUNK)],
                                axis=0)
        for g in range(2):
            gl = 2 * m + g
            lag = _dot_nt(split(bb_re, bb_im, g), z_all, precision=lax.Precision.HIGHEST)
            for n in range(CHUNK):
                src = lag[:, (n // GROUPS_PER_SLAB) * LANES:(n // GROUPS_PER_SLAB + 1) * LANES]
                shift = (SSM_GROUP * (gl - n % GROUPS_PER_SLAB)) % LANES
                moved = pltpu.roll(src, shift, 1) if shift else src
                lag_ref[0, gl, :, n * LANES:(n + 1) * LANES] = jnp.where(lane_block == gl, moved,
                                                                         0.0).astype(bf16)

        def pow_row(p):
            return jnp.concatenate([p[0], p[1]], axis=1)

        cols = slice(m * 2 * PAIR_STATES, (m + 1) * 2 * PAIR_STATES)
        a16 = pows[CHUNK]
        a32 = cmul(a16, a16)
        pow_ref[_ROW_A1:_ROW_A1 + 1, cols] = pow_row(a16)
        pow_ref[_ROW_A2:_ROW_A2 + 1, cols] = pow_row(a32)
        pow_ref[_ROW_A4:_ROW_A4 + 1, cols] = pow_row(cmul(a32, a32))
        carry_pow = a16
        for r in range(SUBLANES):
            pow_ref[_ROW_CARRY + r:_ROW_CARRY + r + 1, cols] = pow_row(carry_pow)
            carry_pow = cmul(carry_pow, a16)


def _ssm_prep(a_re, a_im, log_dt, b_re, b_im, c_re, c_im):
    g, p, h = SSM_GROUPS, SSM_STATE, SSM_GROUP
    slab_lanes = GROUPS_PER_SLAB * p
    flat = lambda v: v.reshape(1, g * p)
    row_spec = pl.BlockSpec((1, slab_lanes), lambda v: (0, v))
    mat_spec = pl.BlockSpec((h, slab_lanes), lambda v: (0, v))
    bf16 = jnp.bfloat16
    compact = (N_SLABS, PAIRS_PER_SLAB, CHUNK, 2, SSM_GROUP, 2 * PAIR_STATES)
    return pl.pallas_call(
        _ssm_prep_kernel,
        grid=(N_SLABS,),
        in_specs=[row_spec, row_spec, row_spec, mat_spec, mat_spec, mat_spec, mat_spec],
        out_specs=(pl.BlockSpec((1, GROUPS_PER_SLAB, SSM_GROUP, SLAB_IN), lambda v: (v, 0, 0, 0)),
                   pl.BlockSpec((1,) + compact[1:], lambda v: (v, 0, 0, 0, 0, 0)),
                   pl.BlockSpec((1,) + compact[1:], lambda v: (v, 0, 0, 0, 0, 0)),
                   pl.BlockSpec((_POW_ROWS, SLAB_STATES), lambda v: (0, v))),
        out_shape=(jax.ShapeDtypeStruct((N_SLABS, GROUPS_PER_SLAB, SSM_GROUP, SLAB_IN), bf16),
                   jax.ShapeDtypeStruct(compact, bf16),
                   jax.ShapeDtypeStruct(compact, bf16),
                   jax.ShapeDtypeStruct((_POW_ROWS, N_SLABS * SLAB_STATES), jnp.float32)),
        name="ssm_prep",
    )(flat(a_re), flat(a_im), flat(jnp.repeat(log_dt, p)),
      b_re.reshape(g * p, h).T, b_im.reshape(g * p, h).T,
      c_re.transpose(1, 0, 2).reshape(h, g * p), c_im.transpose(1, 0, 2).reshape(h, g * p))


def _adaln_kernel(c_ref, w_ref, b_ref, o_ref):
    o_ref[...] = _dot(_silu(c_ref[...]), w_ref[...]) + b_ref[...]


def _adaln(c_pad, w_ada, b_ada):
    rows = c_pad.shape[0]
    return pl.pallas_call(
        _adaln_kernel,
        grid=(3,),
        in_specs=[pl.BlockSpec((rows, D_MODEL), lambda j: (0, 0)),
                  pl.BlockSpec((D_MODEL, D_MODEL), lambda j: (0, j)),
                  pl.BlockSpec((1, D_MODEL), lambda j: (0, j))],
        out_specs=pl.BlockSpec((rows, D_MODEL), lambda j: (0, j)),
        out_shape=jax.ShapeDtypeStruct((rows, 3 * D_MODEL), jnp.float32),
        name="adaln",
    )(c_pad, w_ada, b_ada.reshape(1, 3 * D_MODEL))


def _ssm_in_kernel(x_ref, mod_ref, npre_ref, wu_ref, u_ref, tok_ref):
    sub_chunks = SUB_T // CHUNK
    for sub in range(SSM_IN_BLOCK_T // SUB_T):
        r0 = sub * SUB_T
        hb = _rms_modulate(x_ref[0, r0:r0 + SUB_T, :], npre_ref[...], mod_ref[0, 1:2, :],
                           mod_ref[0, 0:1, :])
        u = _dot(hb.astype(jnp.bfloat16), wu_ref[...])
        for v in range(N_SLABS):
            tok_ref[v, r0:r0 + SUB_T, :] = u[:, v * LANES:(v + 1) * LANES]
        for v in range(N_SLABS):
            for t in range(CHUNK):
                u_ref[0, v, sub * sub_chunks:(sub + 1) * sub_chunks, t * LANES:(t + 1) * LANES] = (
                    tok_ref[v, pl.ds(r0 + t, sub_chunks, stride=CHUNK), :])


def _ssm_in(x, mod, norm_pre, w_u):
    bsz, seq, d = x.shape
    return pl.pallas_call(
        _ssm_in_kernel,
        grid=(bsz, seq // SSM_IN_BLOCK_T),
        in_specs=[pl.BlockSpec((1, SSM_IN_BLOCK_T, d), lambda b, j: (b, j, 0)),
                  pl.BlockSpec((1, 3, d), lambda b, j: (b, 0, 0)),
                  _const_spec((1, d)), _const_spec((d, d))],
        out_specs=pl.BlockSpec((1, N_SLABS, SSM_IN_BLOCK_T // CHUNK, SLAB_IN), lambda b, j: (b, 0, j, 0)),
        out_shape=jax.ShapeDtypeStruct((bsz, N_SLABS, seq // CHUNK, SLAB_IN), jnp.float32),
        scratch_shapes=[pltpu.VMEM((N_SLABS, SSM_IN_BLOCK_T, LANES), jnp.float32)],
        compiler_params=pltpu.CompilerParams(dimension_semantics=("arbitrary", "arbitrary"),
                                             vmem_limit_bytes=VMEM_LIMIT_BYTES),
        name="ssm_in",
    )(x, mod, norm_pre.reshape(1, d), w_u)


def _ssm_core_kernel(u_ref, dskip_ref, lag_ref, bstc_ref, csttc_ref, pow_ref, y_ref,
                     toep_ref, bst_ref, cstt_ref, s_ref, xprev_ref, carry_ref):
    first_visit = (pl.program_id(1) == 0) & (pl.program_id(2) == 0)
    bf16 = jnp.bfloat16
    f32 = jnp.float32

    @pl.when(first_visit)
    def _():
        toep_ref[...] = jnp.zeros_like(toep_ref)
        bst_ref[...] = jnp.zeros_like(bst_ref)
        cstt_ref[...] = jnp.zeros_like(cstt_ref)
        for gl in range(GROUPS_PER_SLAB):
            m, g = gl // 2, gl % 2
            cols = slice(m * 2 * PAIR_STATES, (m + 1) * 2 * PAIR_STATES)
            for s in range(CHUNK):
                rows = slice(s * LANES + gl * SSM_GROUP, s * LANES + (gl + 1) * SSM_GROUP)
                toep_ref[rows, s * LANES:] = lag_ref[0, gl, :, 0:SLAB_IN - s * LANES]
                bst_ref[rows, cols] = bstc_ref[0, m, s, g]
                cstt_ref[rows, cols] = csttc_ref[0, m, s, g]

    @pl.when(pl.program_id(2) == 0)
    def _():
        carry_ref[...] = jnp.zeros_like(carry_ref)

    u = u_ref[0, 0]
    ub = u.astype(bf16)
    s_ref[...] = _dot(ub, bst_ref[...])

    y_in = []
    for nt in range(SLAB_IN // MXU_DIM):
        cols = slice(nt * MXU_DIM, (nt + 1) * MXU_DIM)
        k_hi = (nt + 1) * MXU_DIM
        y_in.append(dskip_ref[0, :, cols] * u[:, cols] + _dot(ub[:, 0:k_hi], toep_ref[0:k_hi, cols]))

    row = lax.broadcasted_iota(jnp.int32, (SUBLANES, PAIR_STATES), 0)
    for m in range(PAIRS_PER_SLAB):
        re = slice(m * 2 * PAIR_STATES, m * 2 * PAIR_STATES + PAIR_STATES)
        im = slice(m * 2 * PAIR_STATES + PAIR_STATES, (m + 1) * 2 * PAIR_STATES)
        bcast = lambda r, lanes: jnp.broadcast_to(pow_ref[r:r + 1, lanes], (SUBLANES, PAIR_STATES))
        steps = [(sh, bcast(r, re), bcast(r, im)) for sh, r in ((1, _ROW_A1), (2, _ROW_A2), (4, _ROW_A4))]
        pc_r = pow_ref[_ROW_CARRY:_ROW_CARRY + SUBLANES, re]
        pc_i = pow_ref[_ROW_CARRY:_ROW_CARRY + SUBLANES, im]
        cr = carry_ref[m, 0]
        ci = carry_ref[m, 1]
        for c0 in range(0, SSM_BLOCK_CHUNKS, 2 * SUBLANES):
            prev = []
            for o in range(2):
                rows = slice(c0 + o * SUBLANES, c0 + (o + 1) * SUBLANES)
                er = s_ref[rows, re]
                ei = s_ref[rows, im]
                for sh, ar, ai in steps:
                    tr = jnp.where(row >= sh, pltpu.roll(er, sh, 0), 0.0)
                    ti = jnp.where(row >= sh, pltpu.roll(ei, sh, 0), 0.0)
                    er, ei = _cmul_add(ar, ai, tr, ti, er, ei)
                er, ei = _cmul_add(pc_r, pc_i, cr, ci, er, ei)
                prev.append((jnp.where(row == 0, cr, pltpu.roll(er, 1, 0)),
                             jnp.where(row == 0, ci, pltpu.roll(ei, 1, 0))))
                cr = jnp.broadcast_to(er[SUBLANES - 1:SUBLANES], (SUBLANES, PAIR_STATES))
                ci = jnp.broadcast_to(ei[SUBLANES - 1:SUBLANES], (SUBLANES, PAIR_STATES))
            rows = slice(c0, c0 + 2 * SUBLANES)
            xprev_ref[rows, re] = jnp.concatenate([prev[0][0], prev[1][0]], axis=0).astype(bf16)
            xprev_ref[rows, im] = jnp.concatenate([prev[0][1], prev[1][1]], axis=0).astype(bf16)
        carry_ref[m, 0] = cr
        carry_ref[m, 1] = ci

    xp = xprev_ref[...]
    for nt in range(SLAB_IN // MXU_DIM):
        y_tile = y_in[nt] + _dot_nt(xp, cstt_ref[nt * MXU_DIM:(nt + 1) * MXU_DIM, :])
        for i in range(MXU_DIM // LANES):
            t = nt * (MXU_DIM // LANES) + i
            y_ref[0, 0, pl.ds(t, SSM_BLOCK_CHUNKS, stride=CHUNK), :] = y_tile[:, i * LANES:(i + 1) * LANES]


def _ssm_core(u_rows, d_skip, lag, bstc, csttc, pow_tab):
    bsz, n_slabs, n_chunks, lanes = u_rows.shape
    seq = n_chunks * CHUNK
    assert seq % SSM_BLOCK_T == 0 and lanes == SLAB_IN
    bf16 = jnp.bfloat16
    f32 = jnp.float32
    tok_spec = pl.BlockSpec((1, 1, SSM_BLOCK_CHUNKS, SLAB_IN), lambda v, b, j: (b, v, j, 0))
    d_rows = jnp.tile(d_skip.reshape(n_slabs, 1, LANES), (1, 1, CHUNK))
    compact = bstc.shape
    return pl.pallas_call(
        _ssm_core_kernel,
        grid=(N_SLABS, bsz, seq // SSM_BLOCK_T),
        in_specs=[tok_spec,
                  pl.BlockSpec((1, 1, SLAB_IN), lambda v, b, j: (v, 0, 0)),
                  pl.BlockSpec((1,) + lag.shape[1:], lambda v, b, j: (v, 0, 0, 0)),
                  pl.BlockSpec((1,) + compact[1:], lambda v, b, j: (v, 0, 0, 0, 0, 0)),
                  pl.BlockSpec((1,) + compact[1:], lambda v, b, j: (v, 0, 0, 0, 0, 0)),
                  pl.BlockSpec((_POW_ROWS, SLAB_STATES), lambda v, b, j: (0, v))],
        out_specs=pl.BlockSpec((1, 1, SSM_BLOCK_T, LANES), lambda v, b, j: (b, v, j, 0)),
        out_shape=jax.ShapeDtypeStruct((bsz, n_slabs, seq, LANES), f32),
        scratch_shapes=[
            pltpu.VMEM((SLAB_IN, SLAB_IN), bf16),
            pltpu.VMEM((SLAB_IN, SLAB_STATES), bf16),
            pltpu.VMEM((SLAB_IN, SLAB_STATES), bf16),
            pltpu.VMEM((SSM_BLOCK_CHUNKS, SLAB_STATES), f32),
            pltpu.VMEM((SSM_BLOCK_CHUNKS, SLAB_STATES), bf16),
            pltpu.VMEM((PAIRS_PER_SLAB, 2, SUBLANES, PAIR_STATES), f32),
        ],
        compiler_params=pltpu.CompilerParams(
            dimension_semantics=("arbitrary", "arbitrary", "arbitrary"),
            vmem_limit_bytes=VMEM_LIMIT_BYTES),
        name="ssm_core",
    )(u_rows, d_rows, lag, bstc, csttc, pow_tab)


def _block_kernel(x_ref, mod_ref, npre_ref, npost_ref, win_ref, ys_ref, poolw_ref, pscale_ref,
                  gluw_ref, glub_ref, wbp_ref, wbs_ref, wout_ref, o_ref, ext_ref):
    j = pl.program_id(1)
    bf16 = jnp.bfloat16

    @pl.when(j == 0)
    def _():
        ext_ref[0:POOL_HALO, :] = jnp.zeros((POOL_HALO, D_MODEL), jnp.float32)

    gate = mod_ref[0, 2:3, :]
    for sub in range(BLOCK_T // SUB_T):
        r0 = sub * SUB_T
        x = x_ref[0, r0:r0 + SUB_T, :]
        hb = _rms_modulate(x, npre_ref[...], mod_ref[0, 1:2, :], mod_ref[0, 0:1, :]).astype(bf16)

        pn = _dot(hb, win_ref[:, 0:2 * D_MODEL])
        z_ssm = _dot(hb, win_ref[:, 3 * D_MODEL:4 * D_MODEL])
        gates = _dot(hb, win_ref[:, 4 * D_MODEL:6 * D_MODEL])
        u_pool = pn[:, 0:D_MODEL]

        e0 = POOL_HALO + r0
        ext_ref[e0:e0 + SUB_T, :] = u_pool
        pos = (j * BLOCK_T + r0 + 1
               + lax.broadcasted_iota(jnp.int32, (SUB_T, 1), 0)).astype(jnp.float32)
        mixed = []
        for g, w in enumerate(POOL_WINDOWS):
            cols = slice(g * POOL_GROUP_WIDTH, (g + 1) * POOL_GROUP_WIDTH)
            wsum = u_pool[:, cols]
            for lag in range(1, w):
                wsum = wsum + ext_ref[e0 - lag:e0 - lag + SUB_T, cols]
            pooled = wsum / jnp.minimum(pos, float(w)) - u_pool[:, cols]
            mixed.append(_dot(pooled.astype(bf16), poolw_ref[g]))
        y_pool = (jnp.concatenate(mixed, axis=-1) * pscale_ref[...]
                  * _silu(pn[:, D_MODEL:2 * D_MODEL]))

        y = _gelu_tanh(jnp.concatenate([ys_ref[0, v, r0:r0 + SUB_T, :] for v in range(N_SLABS)],
                                       axis=-1))
        y = y * _sigmoid(_dot(y.astype(bf16), gluw_ref[...]) + glub_ref[...])
        y_ssm = (y * _silu(z_ssm)).astype(bf16)

        merged = (_sigmoid(gates[:, 0:D_MODEL]) * _dot(y_pool.astype(bf16), wbp_ref[...])
                  + _sigmoid(gates[:, D_MODEL:2 * D_MODEL]) * _dot(y_ssm, wbs_ref[...]))
        out = _dot(merged.astype(bf16), wout_ref[...])
        rn = out * lax.rsqrt(jnp.mean(out * out, axis=-1, keepdims=True) + RMS_EPS) * npost_ref[...]
        o_ref[0, r0:r0 + SUB_T, :] = x + gate * rn
    ext_ref[0:POOL_HALO, :] = ext_ref[BLOCK_T:BLOCK_T + POOL_HALO, :]


def _layer(x, c, w_ada, b_ada, norm_pre, norm_post, w_in, pool_w, pool_scale, a_re, a_im, log_dt,
           b_re, b_im, c_re, c_im, d_skip, glu_w, glu_b, w_branch_pool, w_branch_ssm, w_out):
    bsz, seq, d = x.shape
    assert d == D_MODEL and seq % BLOCK_T == 0
    bf16 = jnp.bfloat16
    f32 = jnp.float32

    lag, bstc, csttc, pow_tab = _ssm_prep(a_re, a_im, log_dt, b_re, b_im, c_re, c_im)
    c_pad = jnp.zeros((SUBLANES, d), f32).at[:bsz].set(c)
    mod = _adaln(c_pad, w_ada, b_ada)[:bsz].reshape(bsz, 3, d)
    w_in_b = w_in.astype(bf16)

    u_rows = _ssm_in(x, mod, norm_pre, w_in_b[:, 2 * d:3 * d])
    y_slabs = _ssm_core(u_rows, d_skip, lag, bstc, csttc, pow_tab)

    row = lambda v: v.reshape(1, d)
    tok_spec = pl.BlockSpec((1, BLOCK_T, d), lambda b, j: (b, j, 0))
    slab_spec = pl.BlockSpec((1, N_SLABS, BLOCK_T, LANES), lambda b, j: (b, 0, j, 0))
    operands = [
        (x, tok_spec),
        (mod, pl.BlockSpec((1, 3, d), lambda b, j: (b, 0, 0))),
        (row(norm_pre), None), (row(norm_post), None),
        (w_in_b, None),
        (y_slabs, slab_spec),
        (pool_w.astype(bf16), None), (row(pool_scale), None),
        (glu_w.astype(bf16), None), (row(glu_b), None),
        (w_branch_pool.astype(bf16), None), (w_branch_ssm.astype(bf16), None),
        (w_out.astype(bf16), None),
    ]
    arrays = [a for a, _ in operands]
    specs = [s if s is not None else _const_spec(a.shape) for a, s in operands]

    return pl.pallas_call(
        _block_kernel,
        grid=(bsz, seq // BLOCK_T),
        in_specs=specs,
        out_specs=tok_spec,
        out_shape=jax.ShapeDtypeStruct((bsz, seq, d), x.dtype),
        scratch_shapes=[pltpu.VMEM((BLOCK_T + POOL_HALO, d), f32)],
        compiler_params=pltpu.CompilerParams(
            dimension_semantics=("arbitrary", "arbitrary"),
            vmem_limit_bytes=VMEM_LIMIT_BYTES),
        name="block",
    )(*arrays)


def kernel(x, c, w_ada, b_ada, norm_pre, norm_post, w_in, pool_w, pool_scale, ssm_a_re, ssm_a_im,
           ssm_log_dt, ssm_b_re, ssm_b_im, ssm_c_re, ssm_c_im, ssm_d, glu_w, glu_b, w_branch_pool,
           w_branch_ssm, w_out):
    for layer in range(w_in.shape[0]):
        x = _layer(x, c, w_ada[layer], b_ada[layer], norm_pre[layer], norm_post[layer],
                   w_in[layer], pool_w[layer], pool_scale[layer], ssm_a_re[layer],
                   ssm_a_im[layer], ssm_log_dt[layer], ssm_b_re[layer], ssm_b_im[layer],
                   ssm_c_re[layer], ssm_c_im[layer], ssm_d.reshape(ssm_d.shape[0], -1)[layer],
                   glu_w[layer], glu_b[layer], w_branch_pool[layer], w_branch_ssm[layer],
                   w_out[layer])
    return x
```

```python
import math

import numpy as np
import jax
import jax.numpy as jnp
from jax import lax
from jax.experimental import pallas as pl
from jax.experimental.pallas import tpu as pltpu

D_MODEL = 1024
POOL_WINDOWS = (2, 4, 8, 16)
POOL_GROUP_WIDTH = D_MODEL // len(POOL_WINDOWS)
POOL_HALO = 16
SSM_GROUP = 16
SSM_GROUPS = D_MODEL // SSM_GROUP
SSM_STATE = 64
RMS_EPS = 1e-6

SUBLANES = 8
LANES = 128
MXU_DIM = 256
CHUNK = 16
GROUPS_PER_SLAB = LANES // SSM_GROUP
N_SLABS = D_MODEL // LANES
PAIRS_PER_SLAB = GROUPS_PER_SLAB // 2
PAIR_STATES = 2 * SSM_STATE
SLAB_IN = CHUNK * LANES
SLAB_STATES = PAIRS_PER_SLAB * 2 * PAIR_STATES
BLOCK_T = 512
SUB_T = 256
SSM_IN_BLOCK_T = 1024
SSM_BLOCK_T = 4096
SSM_BLOCK_CHUNKS = SSM_BLOCK_T // CHUNK
VMEM_LIMIT_BYTES = 56 * 1024 * 1024

_ROW_A1, _ROW_A2, _ROW_A4, _ROW_CARRY = 0, 1, 2, 8
_POW_ROWS = 16


def _sigmoid(v):
    return 0.5 * jnp.tanh(0.5 * v) + 0.5


def _silu(v):
    return v * _sigmoid(v)


def _gelu_tanh(v):
    c = math.sqrt(2.0 / math.pi)
    return 0.5 * v * (1.0 + jnp.tanh(c * (v + 0.044715 * (v * v * v))))


def _dot(a, b):
    return jnp.dot(a, b, preferred_element_type=jnp.float32)


def _dot_nt(a, b, precision=None):
    return lax.dot_general(a, b, (((1,), (1,)), ((), ())), precision=precision,
                           preferred_element_type=jnp.float32)


def _cmul_add(ar, ai, xr, xi, br, bi):
    return ar * xr - ai * xi + br, ar * xi + ai * xr + bi


def _rms_modulate(x, norm_gain, scale, shift):
    xn = x * lax.rsqrt(jnp.mean(x * x, axis=-1, keepdims=True) + RMS_EPS) * norm_gain
    return xn * (1.0 + scale) + shift


def _const_spec(shape):
    nd = len(shape)
    return pl.BlockSpec(shape, lambda *_: (0,) * nd, pipeline_mode=pl.Buffered(1))


def _ssm_prep_kernel(are_ref, aim_ref, ldt_ref, bre_ref, bim_ref, cre_ref, cim_ref,
                     lag_ref, bst_ref, cstt_ref, pow_ref):
    bf16 = jnp.bfloat16
    in_group = [lax.broadcasted_iota(jnp.int32, (SSM_GROUP, LANES), 1) // SSM_STATE == g
                for g in range(2)]
    lane_block = lax.broadcasted_iota(jnp.int32, (SSM_GROUP, LANES), 1) // SSM_GROUP

    def split(v_re, v_im, g):
        return jnp.concatenate([jnp.where(in_group[g], v_re, 0.0), jnp.where(in_group[g], v_im, 0.0)],
                               axis=1)

    pow_ref[...] = jnp.zeros_like(pow_ref)
    for m in range(PAIRS_PER_SLAB):
        lanes = slice(m * LANES, (m + 1) * LANES)
        dt = jnp.exp(ldt_ref[:, lanes])
        lam_re = jnp.minimum(are_ref[:, lanes], -1e-4)
        lam_im = aim_ref[:, lanes]
        l_re = lam_re * dt
        l_im = lam_im * dt

        def cmul(p, q):
            return p[0] * q[0] - p[1] * q[1], p[0] * q[1] + p[1] * q[0]

        mag = jnp.exp(l_re)
        pows = [(jnp.ones_like(l_re), jnp.zeros_like(l_re)), (mag * jnp.cos(l_im), mag * jnp.sin(l_im))]
        for _ in range(CHUNK - 1):
            pows.append(cmul(pows[-1], pows[1]))
        abar_re, abar_im = pows[1]
        den = lam_re * lam_re + lam_im * lam_im
        num_re = abar_re - 1.0
        f_re = (num_re * lam_re + abar_im * lam_im) / den
        f_im = (abar_im * lam_re - num_re * lam_im) / den
        b_re = bre_ref[:, lanes]
        b_im = bim_ref[:, lanes]
        bb_re = f_re * b_re - f_im * b_im
        bb_im = f_re * b_im + f_im * b_re
        c_re = cre_ref[:, lanes]
        c_im = cim_ref[:, lanes]

        for s in range(CHUNK):
            ar, ai = pows[CHUNK - 1 - s]
            w_re = ar * bb_re - ai * bb_im
            w_im = ar * bb_im + ai * bb_re
            for g in range(2):
                bst_ref[0, m, s, g] = split(w_re, w_im, g).astype(bf16)

        z = []
        for n in range(CHUNK + 1):
            ar, ai = pows[n]
            z.append((ar * c_re - ai * c_im, -(ar * c_im + ai * c_re)))
        for t in range(CHUNK):
            for g in range(2):
                cstt_ref[0, m, t, g] = split(z[t + 1][0], z[t + 1][1], g).astype(bf16)

        z_all = jnp.concatenate([jnp.concatenate([z[n][0], z[n][1]], axis=1) for n in range(CHUNK)],
                                axis=0)
        for g in range(2):
            gl = 2 * m + g
            lag = _dot_nt(split(bb_re, bb_im, g), z_all, precision=lax.Precision.HIGHEST)
            for n in range(CHUNK):
                src = lag[:, (n // GROUPS_PER_SLAB) * LANES:(n // GROUPS_PER_SLAB + 1) * LANES]
                shift = (SSM_GROUP * (gl - n % GROUPS_PER_SLAB)) % LANES
                moved = pltpu.roll(src, shift, 1) if shift else src
                lag_ref[0, gl, :, n * LANES:(n + 1) * LANES] = jnp.where(lane_block == gl, moved,
                                                                         0.0).astype(bf16)

        def pow_row(p):
            return jnp.concatenate([p[0], p[1]], axis=1)

        cols = slice(m * 2 * PAIR_STATES, (m + 1) * 2 * PAIR_STATES)
        a16 = pows[CHUNK]
        a32 = cmul(a16, a16)
        pow_ref[_ROW_A1:_ROW_A1 + 1, cols] = pow_row(a16)
        pow_ref[_ROW_A2:_ROW_A2 + 1, cols] = pow_row(a32)
        pow_ref[_ROW_A4:_ROW_A4 + 1, cols] = pow_row(cmul(a32, a32))
        carry_pow = a16
        for r in range(SUBLANES):
            pow_ref[_ROW_CARRY + r:_ROW_CARRY + r + 1, cols] = pow_row(carry_pow)
            carry_pow = cmul(carry_pow, a16)


def _ssm_prep(a_re, a_im, log_dt, b_re, b_im, c_re, c_im):
    g, p, h = SSM_GROUPS, SSM_STATE, SSM_GROUP
    slab_lanes = GROUPS_PER_SLAB * p
    flat = lambda v: v.reshape(1, g * p)
    row_spec = pl.BlockSpec((1, slab_lanes), lambda v: (0, v))
    mat_spec = pl.BlockSpec((h, slab_lanes), lambda v: (0, v))
    bf16 = jnp.bfloat16
    compact = (N_SLABS, PAIRS_PER_SLAB, CHUNK, 2, SSM_GROUP, 2 * PAIR_STATES)
    return pl.pallas_call(
        _ssm_prep_kernel,
        grid=(N_SLABS,),
        in_specs=[row_spec, row_spec, row_spec, mat_spec, mat_spec, mat_spec, mat_spec],
        out_specs=(pl.BlockSpec((1, GROUPS_PER_SLAB, SSM_GROUP, SLAB_IN), lambda v: (v, 0, 0, 0)),
                   pl.BlockSpec((1,) + compact[1:], lambda v: (v, 0, 0, 0, 0, 0)),
                   pl.BlockSpec((1,) + compact[1:], lambda v: (v, 0, 0, 0, 0, 0)),
                   pl.BlockSpec((_POW_ROWS, SLAB_STATES), lambda v: (0, v))),
        out_shape=(jax.ShapeDtypeStruct((N_SLABS, GROUPS_PER_SLAB, SSM_GROUP, SLAB_IN), bf16),
                   jax.ShapeDtypeStruct(compact, bf16),
                   jax.ShapeDtypeStruct(compact, bf16),
                   jax.ShapeDtypeStruct((_POW_ROWS, N_SLABS * SLAB_STATES), jnp.float32)),
        name="ssm_prep",
    )(flat(a_re), flat(a_im), flat(jnp.repeat(log_dt, p)),
      b_re.reshape(g * p, h).T, b_im.reshape(g * p, h).T,
      c_re.transpose(1, 0, 2).reshape(h, g * p), c_im.transpose(1, 0, 2).reshape(h, g * p))


def _adaln_kernel(c_ref, w_ref, b_ref, o_ref):
    o_ref[...] = _dot(_silu(c_ref[...]), w_ref[...]) + b_ref[...]


def _adaln(c_pad, w_ada, b_ada):
    rows = c_pad.shape[0]
    return pl.pallas_call(
        _adaln_kernel,
        grid=(3,),
        in_specs=[pl.BlockSpec((rows, D_MODEL), lambda j: (0, 0)),
                  pl.BlockSpec((D_MODEL, D_MODEL), lambda j: (0, j)),
                  pl.BlockSpec((1, D_MODEL), lambda j: (0, j))],
        out_specs=pl.BlockSpec((rows, D_MODEL), lambda j: (0, j)),
        out_shape=jax.ShapeDtypeStruct((rows, 3 * D_MODEL), jnp.float32),
        name="adaln",
    )(c_pad, w_ada, b_ada.reshape(1, 3 * D_MODEL))


def _ssm_in_kernel(x_ref, mod_ref, npre_ref, perm_ref, wu_ref, u_ref):
    sub_chunks = SUB_T // CHUNK
    for sub in range(SSM_IN_BLOCK_T // SUB_T):
        r0 = sub * SUB_T
        hb = _rms_modulate(x_ref[0, r0:r0 + SUB_T, :], npre_ref[...], mod_ref[0, 1:2, :],
                           mod_ref[0, 0:1, :]).astype(jnp.bfloat16)
        hp = _dot(perm_ref[...], hb).astype(jnp.bfloat16)
        u = _dot(hp, wu_ref[...])
        for v in range(N_SLABS):
            for t in range(CHUNK):
                u_ref[0, v, sub * sub_chunks:(sub + 1) * sub_chunks, t * LANES:(t + 1) * LANES] = (
                    u[t * sub_chunks:(t + 1) * sub_chunks, v * LANES:(v + 1) * LANES])


def _chunk_permutation():
    n_chunks = SUB_T // CHUNK
    p = np.zeros((SUB_T, SUB_T), np.float32)
    for c in range(n_chunks):
        for t in range(CHUNK):
            p[t * n_chunks + c, c * CHUNK + t] = 1.0
    return p


def _ssm_in(x, mod, norm_pre, w_u):
    bsz, seq, d = x.shape
    return pl.pallas_call(
        _ssm_in_kernel,
        grid=(bsz, seq // SSM_IN_BLOCK_T),
        in_specs=[pl.BlockSpec((1, SSM_IN_BLOCK_T, d), lambda b, j: (b, j, 0)),
                  pl.BlockSpec((1, 3, d), lambda b, j: (b, 0, 0)),
                  _const_spec((1, d)), _const_spec((SUB_T, SUB_T)), _const_spec((d, d))],
        out_specs=pl.BlockSpec((1, N_SLABS, SSM_IN_BLOCK_T // CHUNK, SLAB_IN), lambda b, j: (b, 0, j, 0)),
        out_shape=jax.ShapeDtypeStruct((bsz, N_SLABS, seq // CHUNK, SLAB_IN), jnp.float32),
        compiler_params=pltpu.CompilerParams(dimension_semantics=("arbitrary", "arbitrary"),
                                             vmem_limit_bytes=VMEM_LIMIT_BYTES),
        name="ssm_in",
    )(x, mod, norm_pre.reshape(1, d), jnp.asarray(_chunk_permutation(), jnp.bfloat16), w_u)


def _ssm_core_kernel(u_ref, dskip_ref, lag_ref, bstc_ref, csttc_ref, pow_ref, y_ref,
                     toep_ref, bst_ref, cstt_ref, s_ref, xprev_ref, carry_ref):
    first_visit = (pl.program_id(1) == 0) & (pl.program_id(2) == 0)
    bf16 = jnp.bfloat16
    f32 = jnp.float32

    @pl.when(first_visit)
    def _():
        toep_ref[...] = jnp.zeros_like(toep_ref)
        bst_ref[...] = jnp.zeros_like(bst_ref)
        cstt_ref[...] = jnp.zeros_like(cstt_ref)
        for gl in range(GROUPS_PER_SLAB):
            m, g = gl // 2, gl % 2
            cols = slice(m * 2 * PAIR_STATES, (m + 1) * 2 * PAIR_STATES)
            for s in range(CHUNK):
                rows = slice(s * LANES + gl * SSM_GROUP, s * LANES + (gl + 1) * SSM_GROUP)
                toep_ref[rows, s * LANES:] = lag_ref[0, gl, :, 0:SLAB_IN - s * LANES]
                bst_ref[rows, cols] = bstc_ref[0, m, s, g]
                cstt_ref[rows, cols] = csttc_ref[0, m, s, g]

    @pl.when(pl.program_id(2) == 0)
    def _():
        carry_ref[...] = jnp.zeros_like(carry_ref)

    u = u_ref[0, 0]
    ub = u.astype(bf16)
    s_ref[...] = _dot(ub, bst_ref[...])

    y_in = []
    for nt in range(SLAB_IN // MXU_DIM):
        cols = slice(nt * MXU_DIM, (nt + 1) * MXU_DIM)
        k_hi = (nt + 1) * MXU_DIM
        y_in.append(dskip_ref[0, :, cols] * u[:, cols] + _dot(ub[:, 0:k_hi], toep_ref[0:k_hi, cols]))

    row = lax.broadcasted_iota(jnp.int32, (SUBLANES, PAIR_STATES), 0)
    for m in range(PAIRS_PER_SLAB):
        re = slice(m * 2 * PAIR_STATES, m * 2 * PAIR_STATES + PAIR_STATES)
        im = slice(m * 2 * PAIR_STATES + PAIR_STATES, (m + 1) * 2 * PAIR_STATES)
        bcast = lambda r, lanes: jnp.broadcast_to(pow_ref[r:r + 1, lanes], (SUBLANES, PAIR_STATES))
        steps = [(sh, bcast(r, re), bcast(r, im)) for sh, r in ((1, _ROW_A1), (2, _ROW_A2), (4, _ROW_A4))]
        pc_r = pow_ref[_ROW_CARRY:_ROW_CARRY + SUBLANES, re]
        pc_i = pow_ref[_ROW_CARRY:_ROW_CARRY + SUBLANES, im]
        cr = carry_ref[m, 0]
        ci = carry_ref[m, 1]
        for c0 in range(0, SSM_BLOCK_CHUNKS, 2 * SUBLANES):
            prev = []
            for o in range(2):
                rows = slice(c0 + o * SUBLANES, c0 + (o + 1) * SUBLANES)
                er = s_ref[rows, re]
                ei = s_ref[rows, im]
                for sh, ar, ai in steps:
                    tr = jnp.where(row >= sh, pltpu.roll(er, sh, 0), 0.0)
                    ti = jnp.where(row >= sh, pltpu.roll(ei, sh, 0), 0.0)
                    er, ei = _cmul_add(ar, ai, tr, ti, er, ei)
                er, ei = _cmul_add(pc_r, pc_i, cr, ci, er, ei)
                prev.append((jnp.where(row == 0, cr, pltpu.roll(er, 1, 0)),
                             jnp.where(row == 0, ci, pltpu.roll(ei, 1, 0))))
                cr = jnp.broadcast_to(er[SUBLANES - 1:SUBLANES], (SUBLANES, PAIR_STATES))
                ci = jnp.broadcast_to(ei[SUBLANES - 1:SUBLANES], (SUBLANES, PAIR_STATES))
            rows = slice(c0, c0 + 2 * SUBLANES)
            xprev_ref[rows, re] = jnp.concatenate([prev[0][0], prev[1][0]], axis=0).astype(bf16)
            xprev_ref[rows, im] = jnp.concatenate([prev[0][1], prev[1][1]], axis=0).astype(bf16)
        carry_ref[m, 0] = cr
        carry_ref[m, 1] = ci

    xp = xprev_ref[...]
    for nt in range(SLAB_IN // MXU_DIM):
        y_tile = y_in[nt] + _dot_nt(xp, cstt_ref[nt * MXU_DIM:(nt + 1) * MXU_DIM, :])
        for i in range(MXU_DIM // LANES):
            t = nt * (MXU_DIM // LANES) + i
            y_ref[0, 0, pl.ds(t, SSM_BLOCK_CHUNKS, stride=CHUNK), :] = y_tile[:, i * LANES:(i + 1) * LANES]


def _ssm_core(u_rows, d_skip, lag, bstc, csttc, pow_tab):
    bsz, n_slabs, n_chunks, lanes = u_rows.shape
    seq = n_chunks * CHUNK
    assert seq % SSM_BLOCK_T == 0 and lanes == SLAB_IN
    bf16 = jnp.bfloat16
    f32 = jnp.float32
    tok_spec = pl.BlockSpec((1, 1, SSM_BLOCK_CHUNKS, SLAB_IN), lambda v, b, j: (b, v, j, 0))
    d_rows = jnp.tile(d_skip.reshape(n_slabs, 1, LANES), (1, 1, CHUNK))
    compact = bstc.shape
    return pl.pallas_call(
        _ssm_core_kernel,
        grid=(N_SLABS, bsz, seq // SSM_BLOCK_T),
        in_specs=[tok_spec,
                  pl.BlockSpec((1, 1, SLAB_IN), lambda v, b, j: (v, 0, 0)),
                  pl.BlockSpec((1,) + lag.shape[1:], lambda v, b, j: (v, 0, 0, 0)),
                  pl.BlockSpec((1,) + compact[1:], lambda v, b, j: (v, 0, 0, 0, 0, 0)),
                  pl.BlockSpec((1,) + compact[1:], lambda v, b, j: (v, 0, 0, 0, 0, 0)),
                  pl.BlockSpec((_POW_ROWS, SLAB_STATES), lambda v, b, j: (0, v))],
        out_specs=pl.BlockSpec((1, 1, SSM_BLOCK_T, LANES), lambda v, b, j: (b, v, j, 0)),
        out_shape=jax.ShapeDtypeStruct((bsz, n_slabs, seq, LANES), f32),
        scratch_shapes=[
            pltpu.VMEM((SLAB_IN, SLAB_IN), bf16),
            pltpu.VMEM((SLAB_IN, SLAB_STATES), bf16),
            pltpu.VMEM((SLAB_IN, SLAB_STATES), bf16),
            pltpu.VMEM((SSM_BLOCK_CHUNKS, SLAB_STATES), f32),
            pltpu.VMEM((SSM_BLOCK_CHUNKS, SLAB_STATES), bf16),
            pltpu.VMEM((PAIRS_PER_SLAB, 2, SUBLANES, PAIR_STATES), f32),
        ],
        compiler_params=pltpu.CompilerParams(
            dimension_semantics=("arbitrary", "arbitrary", "arbitrary"),
            vmem_limit_bytes=VMEM_LIMIT_BYTES),
        name="ssm_core",
    )(u_rows, d_rows, lag, bstc, csttc, pow_tab)


def _block_kernel(x_ref, mod_ref, npre_ref, npost_ref, win_ref, ys_ref, poolw_ref, pscale_ref,
                  gluw_ref, glub_ref, wbp_ref, wbs_ref, wout_ref, o_ref, ext_ref):
    j = pl.program_id(1)
    bf16 = jnp.bfloat16

    @pl.when(j == 0)
    def _():
        ext_ref[0:POOL_HALO, :] = jnp.zeros((POOL_HALO, D_MODEL), jnp.float32)

    gate = mod_ref[0, 2:3, :]
    for sub in range(BLOCK_T // SUB_T):
        r0 = sub * SUB_T
        x = x_ref[0, r0:r0 + SUB_T, :]
        hb = _rms_modulate(x, npre_ref[...], mod_ref[0, 1:2, :], mod_ref[0, 0:1, :]).astype(bf16)

        pn = _dot(hb, win_ref[:, 0:2 * D_MODEL])
        z_ssm = _dot(hb, win_ref[:, 3 * D_MODEL:4 * D_MODEL])
        gates = _dot(hb, win_ref[:, 4 * D_MODEL:6 * D_MODEL])
        u_pool = pn[:, 0:D_MODEL]

        e0 = POOL_HALO + r0
        ext_ref[e0:e0 + SUB_T, :] = u_pool
        pos = (j * BLOCK_T + r0 + 1
               + lax.broadcasted_iota(jnp.int32, (SUB_T, 1), 0)).astype(jnp.float32)
        mixed = []
        for g, w in enumerate(POOL_WINDOWS):
            cols = slice(g * POOL_GROUP_WIDTH, (g + 1) * POOL_GROUP_WIDTH)
            wsum = u_pool[:, cols]
            for lag in range(1, w):
                wsum = wsum + ext_ref[e0 - lag:e0 - lag + SUB_T, cols]
            pooled = wsum / jnp.minimum(pos, float(w)) - u_pool[:, cols]
            mixed.append(_dot(pooled.astype(bf16), poolw_ref[g]))
        y_pool = (jnp.concatenate(mixed, axis=-1) * pscale_ref[...]
                  * _silu(pn[:, D_MODEL:2 * D_MODEL]))

        y = _gelu_tanh(jnp.concatenate([ys_ref[0, v, r0:r0 + SUB_T, :] for v in range(N_SLABS)],
                                       axis=-1))
        y = y * _sigmoid(_dot(y.astype(bf16), gluw_ref[...]) + glub_ref[...])
        y_ssm = (y * _silu(z_ssm)).astype(bf16)

        merged = (_sigmoid(gates[:, 0:D_MODEL]) * _dot(y_pool.astype(bf16), wbp_ref[...])
                  + _sigmoid(gates[:, D_MODEL:2 * D_MODEL]) * _dot(y_ssm, wbs_ref[...]))
        out = _dot(merged.astype(bf16), wout_ref[...])
        rn = out * lax.rsqrt(jnp.mean(out * out, axis=-1, keepdims=True) + RMS_EPS) * npost_ref[...]
        o_ref[0, r0:r0 + SUB_T, :] = x + gate * rn
    ext_ref[0:POOL_HALO, :] = ext_ref[BLOCK_T:BLOCK_T + POOL_HALO, :]


def _layer(x, c, w_ada, b_ada, norm_pre, norm_post, w_in, pool_w, pool_scale, a_re, a_im, log_dt,
           b_re, b_im, c_re, c_im, d_skip, glu_w, glu_b, w_branch_pool, w_branch_ssm, w_out):
    bsz, seq, d = x.shape
    assert d == D_MODEL and seq % BLOCK_T == 0
    bf16 = jnp.bfloat16
    f32 = jnp.float32

    lag, bstc, csttc, pow_tab = _ssm_prep(a_re, a_im, log_dt, b_re, b_im, c_re, c_im)
    c_pad = jnp.zeros((SUBLANES, d), f32).at[:bsz].set(c)
    mod = _adaln(c_pad, w_ada, b_ada)[:bsz].reshape(bsz, 3, d)
    w_in_b = w_in.astype(bf16)

    u_rows = _ssm_in(x, mod, norm_pre, w_in_b[:, 2 * d:3 * d])
    y_slabs = _ssm_core(u_rows, d_skip, lag, bstc, csttc, pow_tab)

    row = lambda v: v.reshape(1, d)
    tok_spec = pl.BlockSpec((1, BLOCK_T, d), lambda b, j: (b, j, 0))
    slab_spec = pl.BlockSpec((1, N_SLABS, BLOCK_T, LANES), lambda b, j: (b, 0, j, 0))
    operands = [
        (x, tok_spec),
        (mod, pl.BlockSpec((1, 3, d), lambda b, j: (b, 0, 0))),
        (row(norm_pre), None), (row(norm_post), None),
        (w_in_b, None),
        (y_slabs, slab_spec),
        (pool_w.astype(bf16), None), (row(pool_scale), None),
        (glu_w.astype(bf16), None), (row(glu_b), None),
        (w_branch_pool.astype(bf16), None), (w_branch_ssm.astype(bf16), None),
        (w_out.astype(bf16), None),
    ]
    arrays = [a for a, _ in operands]
    specs = [s if s is not None else _const_spec(a.shape) for a, s in operands]

    return pl.pallas_call(
        _block_kernel,
        grid=(bsz, seq // BLOCK_T),
        in_specs=specs,
        out_specs=tok_spec,
        out_shape=jax.ShapeDtypeStruct((bsz, seq, d), x.dtype),
        scratch_shapes=[pltpu.VMEM((BLOCK_T + POOL_HALO, d), f32)],
        compiler_params=pltpu.CompilerParams(
            dimension_semantics=("arbitrary", "arbitrary"),
            vmem_limit_bytes=VMEM_LIMIT_BYTES),
        name="block",
    )(*arrays)


def kernel(x, c, w_ada, b_ada, norm_pre, norm_post, w_in, pool_w, pool_scale, ssm_a_re, ssm_a_im,
           ssm_log_dt, ssm_b_re, ssm_b_im, ssm_c_re, ssm_c_im, ssm_d, glu_w, glu_b, w_branch_pool,
           w_branch_ssm, w_out):
    for layer in range(w_in.shape[0]):
        x = _layer(x, c, w_ada[layer], b_ada[layer], norm_pre[layer], norm_post[layer],
                   w_in[layer], pool_w[layer], pool_scale[layer], ssm_a_re[layer],
                   ssm_a_im[layer], ssm_log_dt[layer], ssm_b_re[layer], ssm_b_im[layer],
                   ssm_c_re[layer], ssm_c_im[layer], ssm_d.reshape(ssm_d.shape[0], -1)[layer],
                   glu_w[layer], glu_b[layer], w_branch_pool[layer], w_branch_ssm[layer],
                   w_out[layer])
    return x
```

```python
import math

import numpy as np
import jax
import jax.numpy as jnp
from jax import lax
from jax.experimental import pallas as pl
from jax.experimental.pallas import tpu as pltpu

D_MODEL = 1024
POOL_WINDOWS = (2, 4, 8, 16)
POOL_GROUP_WIDTH = D_MODEL // len(POOL_WINDOWS)
POOL_HALO = 16
SSM_GROUP = 16
SSM_GROUPS = D_MODEL // SSM_GROUP
SSM_STATE = 64
RMS_EPS = 1e-6

SUBLANES = 8
LANES = 128
MXU_DIM = 256
CHUNK = 16
GROUPS_PER_SLAB = LANES // SSM_GROUP
N_SLABS = D_MODEL // LANES
PAIRS_PER_SLAB = GROUPS_PER_SLAB // 2
PAIR_STATES = 2 * SSM_STATE
SLAB_IN = CHUNK * LANES
PAIR_LANES = 2 * SSM_GROUP
PAIR_IN = CHUNK * PAIR_LANES
QUAD = LANES // PAIR_LANES
SLAB_STATES = PAIRS_PER_SLAB * 2 * PAIR_STATES
BLOCK_T = 512
SUB_T = 256
SSM_IN_BLOCK_T = 1024
SSM_BLOCK_T = 4096
SSM_BLOCK_CHUNKS = SSM_BLOCK_T // CHUNK
VMEM_LIMIT_BYTES = 56 * 1024 * 1024

_ROW_A1, _ROW_A2, _ROW_A4, _ROW_CARRY = 0, 1, 2, 8
_POW_ROWS = 16


def _sigmoid(v):
    return 0.5 * jnp.tanh(0.5 * v) + 0.5


def _silu(v):
    return v * _sigmoid(v)


def _gelu_tanh(v):
    c = math.sqrt(2.0 / math.pi)
    return 0.5 * v * (1.0 + jnp.tanh(c * (v + 0.044715 * (v * v * v))))


def _dot(a, b):
    return jnp.dot(a, b, preferred_element_type=jnp.float32)


def _dot_nt(a, b, precision=None):
    return lax.dot_general(a, b, (((1,), (1,)), ((), ())), precision=precision,
                           preferred_element_type=jnp.float32)


def _cmul_add(ar, ai, xr, xi, br, bi):
    return ar * xr - ai * xi + br, ar * xi + ai * xr + bi


def _rms_modulate(x, norm_gain, scale, shift):
    xn = x * lax.rsqrt(jnp.mean(x * x, axis=-1, keepdims=True) + RMS_EPS) * norm_gain
    return xn * (1.0 + scale) + shift


def _const_spec(shape):
    nd = len(shape)
    return pl.BlockSpec(shape, lambda *_: (0,) * nd, pipeline_mode=pl.Buffered(1))


def _ssm_prep_kernel(are_ref, aim_ref, ldt_ref, bre_ref, bim_ref, cre_ref, cim_ref,
                     toep_ref, bst_ref, cstt_ref, pow_ref):
    bf16 = jnp.bfloat16
    in_group = [lax.broadcasted_iota(jnp.int32, (SSM_GROUP, LANES), 1) // SSM_STATE == g
                for g in range(2)]
    lane_block = lax.broadcasted_iota(jnp.int32, (SSM_GROUP, LANES), 1) // SSM_GROUP

    def split(v_re, v_im, g):
        return jnp.concatenate([jnp.where(in_group[g], v_re, 0.0), jnp.where(in_group[g], v_im, 0.0)],
                               axis=1)

    pow_ref[...] = jnp.zeros_like(pow_ref)
    for m in range(PAIRS_PER_SLAB):
        lanes = slice(m * LANES, (m + 1) * LANES)
        dt = jnp.exp(ldt_ref[:, lanes])
        lam_re = jnp.minimum(are_ref[:, lanes], -1e-4)
        lam_im = aim_ref[:, lanes]
        l_re = lam_re * dt
        l_im = lam_im * dt

        def cmul(p, q):
            return p[0] * q[0] - p[1] * q[1], p[0] * q[1] + p[1] * q[0]

        mag = jnp.exp(l_re)
        pows = [(jnp.ones_like(l_re), jnp.zeros_like(l_re)), (mag * jnp.cos(l_im), mag * jnp.sin(l_im))]
        for _ in range(CHUNK - 1):
            pows.append(cmul(pows[-1], pows[1]))
        abar_re, abar_im = pows[1]
        den = lam_re * lam_re + lam_im * lam_im
        num_re = abar_re - 1.0
        f_re = (num_re * lam_re + abar_im * lam_im) / den
        f_im = (abar_im * lam_re - num_re * lam_im) / den
        b_re = bre_ref[:, lanes]
        b_im = bim_ref[:, lanes]
        bb_re = f_re * b_re - f_im * b_im
        bb_im = f_re * b_im + f_im * b_re
        c_re = cre_ref[:, lanes]
        c_im = cim_ref[:, lanes]

        for s in range(CHUNK):
            ar, ai = pows[CHUNK - 1 - s]
            w_re = ar * bb_re - ai * bb_im
            w_im = ar * bb_im + ai * bb_re
            for g in range(2):
                r0 = s * PAIR_LANES + g * SSM_GROUP
                bst_ref[0, m, r0:r0 + SSM_GROUP, :] = split(w_re, w_im, g).astype(bf16)

        z = []
        for n in range(CHUNK + 1):
            ar, ai = pows[n]
            z.append((ar * c_re - ai * c_im, -(ar * c_im + ai * c_re)))
        for t in range(CHUNK):
            for g in range(2):
                r0 = t * PAIR_LANES + g * SSM_GROUP
                cstt_ref[0, m, r0:r0 + SSM_GROUP, :] = split(z[t + 1][0], z[t + 1][1], g).astype(bf16)

        z_all = jnp.concatenate([jnp.concatenate([z[n][0], z[n][1]], axis=1) for n in range(CHUNK)],
                                axis=0)
        lane_pair = lax.broadcasted_iota(jnp.int32, (SSM_GROUP, PAIR_IN), 1)
        for g in range(2):
            lag = _dot_nt(split(bb_re, bb_im, g), z_all, precision=lax.Precision.HIGHEST)
            cols = []
            for q in range(CHUNK // QUAD):
                col = jnp.zeros((SSM_GROUP, LANES), jnp.float32)
                for i in range(QUAD):
                    n = q * QUAD + i
                    src = lag[:, (n // GROUPS_PER_SLAB) * LANES:(n // GROUPS_PER_SLAB + 1) * LANES]
                    dst_block = 2 * i + g
                    shift = (SSM_GROUP * (dst_block - n % GROUPS_PER_SLAB)) % LANES
                    moved = pltpu.roll(src, shift, 1) if shift else src
                    col = jnp.where(lane_block == dst_block, moved, col)
                cols.append(col)
            lag_pair = jnp.concatenate(cols, axis=1)
            for s in range(CHUNK):
                r0 = s * PAIR_LANES + g * SSM_GROUP
                if s == 0:
                    blk = lag_pair
                else:
                    blk = jnp.where(lane_pair >= s * PAIR_LANES,
                                    pltpu.roll(lag_pair, s * PAIR_LANES, 1), 0.0)
                toep_ref[0, m, r0:r0 + SSM_GROUP, :] = blk.astype(bf16)

        def pow_row(p):
            return jnp.concatenate([p[0], p[1]], axis=1)

        cols = slice(m * 2 * PAIR_STATES, (m + 1) * 2 * PAIR_STATES)
        a16 = pows[CHUNK]
        a32 = cmul(a16, a16)
        pow_ref[_ROW_A1:_ROW_A1 + 1, cols] = pow_row(a16)
        pow_ref[_ROW_A2:_ROW_A2 + 1, cols] = pow_row(a32)
        pow_ref[_ROW_A4:_ROW_A4 + 1, cols] = pow_row(cmul(a32, a32))
        carry_pow = a16
        for r in range(SUBLANES):
            pow_ref[_ROW_CARRY + r:_ROW_CARRY + r + 1, cols] = pow_row(carry_pow)
            carry_pow = cmul(carry_pow, a16)


def _ssm_prep(a_re, a_im, log_dt, b_re, b_im, c_re, c_im):
    g, p, h = SSM_GROUPS, SSM_STATE, SSM_GROUP
    slab_lanes = GROUPS_PER_SLAB * p
    flat = lambda v: v.reshape(1, g * p)
    row_spec = pl.BlockSpec((1, slab_lanes), lambda v: (0, v))
    mat_spec = pl.BlockSpec((h, slab_lanes), lambda v: (0, v))
    bf16 = jnp.bfloat16
    states = (N_SLABS, PAIRS_PER_SLAB, PAIR_IN, 2 * PAIR_STATES)
    toep = (N_SLABS, PAIRS_PER_SLAB, PAIR_IN, PAIR_IN)
    return pl.pallas_call(
        _ssm_prep_kernel,
        grid=(N_SLABS,),
        in_specs=[row_spec, row_spec, row_spec, mat_spec, mat_spec, mat_spec, mat_spec],
        out_specs=(pl.BlockSpec((1,) + toep[1:], lambda v: (v, 0, 0, 0)),
                   pl.BlockSpec((1,) + states[1:], lambda v: (v, 0, 0, 0)),
                   pl.BlockSpec((1,) + states[1:], lambda v: (v, 0, 0, 0)),
                   pl.BlockSpec((_POW_ROWS, SLAB_STATES), lambda v: (0, v))),
        out_shape=(jax.ShapeDtypeStruct(toep, bf16),
                   jax.ShapeDtypeStruct(states, bf16),
                   jax.ShapeDtypeStruct(states, bf16),
                   jax.ShapeDtypeStruct((_POW_ROWS, N_SLABS * SLAB_STATES), jnp.float32)),
        name="ssm_prep",
    )(flat(a_re), flat(a_im), flat(jnp.repeat(log_dt, p)),
      b_re.reshape(g * p, h).T, b_im.reshape(g * p, h).T,
      c_re.transpose(1, 0, 2).reshape(h, g * p), c_im.transpose(1, 0, 2).reshape(h, g * p))


def _adaln_kernel(c_ref, w_ref, b_ref, o_ref):
    o_ref[...] = _dot(_silu(c_ref[...]), w_ref[...]) + b_ref[...]


def _adaln(c_pad, w_ada, b_ada):
    rows = c_pad.shape[0]
    return pl.pallas_call(
        _adaln_kernel,
        grid=(3,),
        in_specs=[pl.BlockSpec((rows, D_MODEL), lambda j: (0, 0)),
                  pl.BlockSpec((D_MODEL, D_MODEL), lambda j: (0, j)),
                  pl.BlockSpec((1, D_MODEL), lambda j: (0, j))],
        out_specs=pl.BlockSpec((rows, D_MODEL), lambda j: (0, j)),
        out_shape=jax.ShapeDtypeStruct((rows, 3 * D_MODEL), jnp.float32),
        name="adaln",
    )(c_pad, w_ada, b_ada.reshape(1, 3 * D_MODEL))


def _ssm_in_kernel(x_ref, mod_ref, npre_ref, perm_ref, wu_ref, u_ref):
    sub_chunks = SUB_T // CHUNK
    for sub in range(SSM_IN_BLOCK_T // SUB_T):
        r0 = sub * SUB_T
        hb = _rms_modulate(x_ref[0, r0:r0 + SUB_T, :], npre_ref[...], mod_ref[0, 1:2, :],
                           mod_ref[0, 0:1, :]).astype(jnp.bfloat16)
        hp = _dot(perm_ref[...], hb).astype(jnp.bfloat16)
        u = _dot(hp, wu_ref[...])
        for v in range(N_SLABS):
            for t in range(CHUNK):
                u_ref[0, v, sub * sub_chunks:(sub + 1) * sub_chunks, t * LANES:(t + 1) * LANES] = (
                    u[t * sub_chunks:(t + 1) * sub_chunks, v * LANES:(v + 1) * LANES])


def _chunk_permutation():
    n_chunks = SUB_T // CHUNK
    p = np.zeros((SUB_T, SUB_T), np.float32)
    for c in range(n_chunks):
        for t in range(CHUNK):
            p[t * n_chunks + c, c * CHUNK + t] = 1.0
    return p


def _ssm_in(x, mod, norm_pre, w_u):
    bsz, seq, d = x.shape
    return pl.pallas_call(
        _ssm_in_kernel,
        grid=(bsz, seq // SSM_IN_BLOCK_T),
        in_specs=[pl.BlockSpec((1, SSM_IN_BLOCK_T, d), lambda b, j: (b, j, 0)),
                  pl.BlockSpec((1, 3, d), lambda b, j: (b, 0, 0)),
                  _const_spec((1, d)), _const_spec((SUB_T, SUB_T)), _const_spec((d, d))],
        out_specs=pl.BlockSpec((1, N_SLABS, SSM_IN_BLOCK_T // CHUNK, SLAB_IN), lambda b, j: (b, 0, j, 0)),
        out_shape=jax.ShapeDtypeStruct((bsz, N_SLABS, seq // CHUNK, SLAB_IN), jnp.float32),
        compiler_params=pltpu.CompilerParams(dimension_semantics=("arbitrary", "arbitrary"),
                                             vmem_limit_bytes=VMEM_LIMIT_BYTES),
        name="ssm_in",
    )(x, mod, norm_pre.reshape(1, d), jnp.asarray(_chunk_permutation(), jnp.bfloat16), w_u)


def _swap_stage(tiles, d):
    block = lax.broadcasted_iota(jnp.int32, tiles[0].shape, 1) // PAIR_LANES
    keep = (block & d) == 0
    out = list(tiles)
    for i in range(PAIRS_PER_SLAB):
        if i & d:
            continue
        a, b = tiles[i], tiles[i + d]
        out[i] = jnp.where(keep, a, pltpu.roll(b, d * PAIR_LANES, 1))
        out[i + d] = jnp.where(keep, pltpu.roll(a, LANES - d * PAIR_LANES, 1), b)
    return out


def _ssm_core_kernel(u_ref, dskip_ref, toep_ref, bst_ref, cstt_ref, pow_ref, y_ref,
                     stage_in_ref, up_ref, s_ref, xprev_ref, yp_ref, stage_out_ref, carry_ref):
    bf16 = jnp.bfloat16
    n_quads = CHUNK // QUAD
    tile = lambda k: slice(k * LANES, (k + 1) * LANES)
    rows16 = 2 * SUBLANES

    @pl.when(pl.program_id(2) == 0)
    def _():
        carry_ref[...] = jnp.zeros_like(carry_ref)

    for c0 in range(0, SSM_BLOCK_CHUNKS, rows16):
        rows = slice(c0, c0 + rows16)
        for q in range(n_quads):
            out = _swap_stage([u_ref[0, 0, rows, tile(q * QUAD + i)].astype(bf16) for i in range(QUAD)], 2)
            for i in range(QUAD):
                stage_in_ref[rows, tile(q * QUAD + i)] = out[i]
    for c0 in range(0, SSM_BLOCK_CHUNKS, rows16):
        rows = slice(c0, c0 + rows16)
        for q in range(n_quads):
            out = _swap_stage([stage_in_ref[rows, tile(q * QUAD + i)] for i in range(QUAD)], 1)
            for m in range(PAIRS_PER_SLAB):
                up_ref[m, rows, tile(q)] = out[m]

    for m in range(PAIRS_PER_SLAB):
        up = up_ref[m]
        s_ref[:, m * 2 * PAIR_STATES:(m + 1) * 2 * PAIR_STATES] = _dot(up, bst_ref[0, m])
        for nt in range(PAIR_IN // MXU_DIM):
            cols = slice(nt * MXU_DIM, (nt + 1) * MXU_DIM)
            k_hi = (nt + 1) * MXU_DIM
            yp_ref[m, :, cols] = _dot(up[:, 0:k_hi], toep_ref[0, m, 0:k_hi, cols])

    row = lax.broadcasted_iota(jnp.int32, (SUBLANES, PAIR_STATES), 0)
    for m in range(PAIRS_PER_SLAB):
        re = slice(m * 2 * PAIR_STATES, m * 2 * PAIR_STATES + PAIR_STATES)
        im = slice(m * 2 * PAIR_STATES + PAIR_STATES, (m + 1) * 2 * PAIR_STATES)
        bcast = lambda r, lanes: jnp.broadcast_to(pow_ref[r:r + 1, lanes], (SUBLANES, PAIR_STATES))
        steps = [(sh, bcast(r, re), bcast(r, im)) for sh, r in ((1, _ROW_A1), (2, _ROW_A2), (4, _ROW_A4))]
        pc_r = pow_ref[_ROW_CARRY:_ROW_CARRY + SUBLANES, re]
        pc_i = pow_ref[_ROW_CARRY:_ROW_CARRY + SUBLANES, im]
        cr = carry_ref[m, 0]
        ci = carry_ref[m, 1]
        for c0 in range(0, SSM_BLOCK_CHUNKS, rows16):
            prev = []
            for o in range(2):
                rows = slice(c0 + o * SUBLANES, c0 + (o + 1) * SUBLANES)
                er = s_ref[rows, re]
                ei = s_ref[rows, im]
                for sh, ar, ai in steps:
                    tr = jnp.where(row >= sh, pltpu.roll(er, sh, 0), 0.0)
                    ti = jnp.where(row >= sh, pltpu.roll(ei, sh, 0), 0.0)
                    er, ei = _cmul_add(ar, ai, tr, ti, er, ei)
                er, ei = _cmul_add(pc_r, pc_i, cr, ci, er, ei)
                prev.append((jnp.where(row == 0, cr, pltpu.roll(er, 1, 0)),
                             jnp.where(row == 0, ci, pltpu.roll(ei, 1, 0))))
                cr = jnp.broadcast_to(er[SUBLANES - 1:SUBLANES], (SUBLANES, PAIR_STATES))
                ci = jnp.broadcast_to(ei[SUBLANES - 1:SUBLANES], (SUBLANES, PAIR_STATES))
            rows = slice(c0, c0 + rows16)
            xprev_ref[rows, re] = jnp.concatenate([prev[0][0], prev[1][0]], axis=0).astype(bf16)
            xprev_ref[rows, im] = jnp.concatenate([prev[0][1], prev[1][1]], axis=0).astype(bf16)
        carry_ref[m, 0] = cr
        carry_ref[m, 1] = ci

    for m in range(PAIRS_PER_SLAB):
        yp_ref[m] += _dot_nt(xprev_ref[:, m * 2 * PAIR_STATES:(m + 1) * 2 * PAIR_STATES], cstt_ref[0, m])

    for c0 in range(0, SSM_BLOCK_CHUNKS, SUBLANES):
        rows = slice(c0, c0 + SUBLANES)
        for q in range(n_quads):
            out = _swap_stage([yp_ref[m, rows, tile(q)] for m in range(PAIRS_PER_SLAB)], 2)
            for m in range(PAIRS_PER_SLAB):
                stage_out_ref[m, rows, tile(q)] = out[m]
    for c0 in range(0, SSM_BLOCK_CHUNKS, SUBLANES):
        rows = slice(c0, c0 + SUBLANES)
        for q in range(n_quads):
            out = _swap_stage([stage_out_ref[m, rows, tile(q)] for m in range(PAIRS_PER_SLAB)], 1)
            for i in range(QUAD):
                t = q * QUAD + i
                y_ref[0, 0, pl.ds(c0 * CHUNK + t, SUBLANES, stride=CHUNK), :] = (
                    out[i] + dskip_ref[0, :, tile(t)] * u_ref[0, 0, rows, tile(t)])


def _ssm_core(u_rows, d_skip, toep, bst, cstt, pow_tab):
    bsz, n_slabs, n_chunks, lanes = u_rows.shape
    seq = n_chunks * CHUNK
    assert seq % SSM_BLOCK_T == 0 and lanes == SLAB_IN
    bf16 = jnp.bfloat16
    f32 = jnp.float32
    tok_spec = pl.BlockSpec((1, 1, SSM_BLOCK_CHUNKS, SLAB_IN), lambda v, b, j: (b, v, j, 0))
    d_rows = jnp.tile(d_skip.reshape(n_slabs, 1, LANES), (1, 1, CHUNK))
    return pl.pallas_call(
        _ssm_core_kernel,
        grid=(N_SLABS, bsz, seq // SSM_BLOCK_T),
        in_specs=[tok_spec,
                  pl.BlockSpec((1, 1, SLAB_IN), lambda v, b, j: (v, 0, 0)),
                  pl.BlockSpec((1,) + toep.shape[1:], lambda v, b, j: (v, 0, 0, 0)),
                  pl.BlockSpec((1,) + bst.shape[1:], lambda v, b, j: (v, 0, 0, 0)),
                  pl.BlockSpec((1,) + cstt.shape[1:], lambda v, b, j: (v, 0, 0, 0)),
                  pl.BlockSpec((_POW_ROWS, SLAB_STATES), lambda v, b, j: (0, v))],
        out_specs=pl.BlockSpec((1, 1, SSM_BLOCK_T, LANES), lambda v, b, j: (b, v, j, 0)),
        out_shape=jax.ShapeDtypeStruct((bsz, n_slabs, seq, LANES), f32),
        scratch_shapes=[
            pltpu.VMEM((SSM_BLOCK_CHUNKS, SLAB_IN), bf16),
            pltpu.VMEM((PAIRS_PER_SLAB, SSM_BLOCK_CHUNKS, PAIR_IN), bf16),
            pltpu.VMEM((SSM_BLOCK_CHUNKS, SLAB_STATES), f32),
            pltpu.VMEM((SSM_BLOCK_CHUNKS, SLAB_STATES), bf16),
            pltpu.VMEM((PAIRS_PER_SLAB, SSM_BLOCK_CHUNKS, PAIR_IN), f32),
            pltpu.VMEM((PAIRS_PER_SLAB, SSM_BLOCK_CHUNKS, PAIR_IN), f32),
            pltpu.VMEM((PAIRS_PER_SLAB, 2, SUBLANES, PAIR_STATES), f32),
        ],
        compiler_params=pltpu.CompilerParams(
            dimension_semantics=("arbitrary", "arbitrary", "arbitrary"),
            vmem_limit_bytes=VMEM_LIMIT_BYTES),
        name="ssm_core",
    )(u_rows, d_rows, toep, bst, cstt, pow_tab)


def _block_kernel(x_ref, mod_ref, npre_ref, npost_ref, win_ref, ys_ref, poolw_ref, pscale_ref,
                  gluw_ref, glub_ref, wbp_ref, wbs_ref, wout_ref, o_ref, ext_ref):
    j = pl.program_id(1)
    bf16 = jnp.bfloat16

    @pl.when(j == 0)
    def _():
        ext_ref[0:POOL_HALO, :] = jnp.zeros((POOL_HALO, D_MODEL), jnp.float32)

    gate = mod_ref[0, 2:3, :]
    for sub in range(BLOCK_T // SUB_T):
        r0 = sub * SUB_T
        x = x_ref[0, r0:r0 + SUB_T, :]
        hb = _rms_modulate(x, npre_ref[...], mod_ref[0, 1:2, :], mod_ref[0, 0:1, :]).astype(bf16)

        pn = _dot(hb, win_ref[:, 0:2 * D_MODEL])
        z_ssm = _dot(hb, win_ref[:, 3 * D_MODEL:4 * D_MODEL])
        gates = _dot(hb, win_ref[:, 4 * D_MODEL:6 * D_MODEL])
        u_pool = pn[:, 0:D_MODEL]

        e0 = POOL_HALO + r0
        ext_ref[e0:e0 + SUB_T, :] = u_pool
        pos = (j * BLOCK_T + r0 + 1
               + lax.broadcasted_iota(jnp.int32, (SUB_T, 1), 0)).astype(jnp.float32)
        mixed = []
        for g, w in enumerate(POOL_WINDOWS):
            cols = slice(g * POOL_GROUP_WIDTH, (g + 1) * POOL_GROUP_WIDTH)
            wsum = u_pool[:, cols]
            for lag in range(1, w):
                wsum = wsum + ext_ref[e0 - lag:e0 - lag + SUB_T, cols]
            pooled = wsum / jnp.minimum(pos, float(w)) - u_pool[:, cols]
            mixed.append(_dot(pooled.astype(bf16), poolw_ref[g]))
        y_pool = (jnp.concatenate(mixed, axis=-1) * pscale_ref[...]
                  * _silu(pn[:, D_MODEL:2 * D_MODEL]))

        y = _gelu_tanh(jnp.concatenate([ys_ref[0, v, r0:r0 + SUB_T, :] for v in range(N_SLABS)],
                                       axis=-1))
        y = y * _sigmoid(_dot(y.astype(bf16), gluw_ref[...]) + glub_ref[...])
        y_ssm = (y * _silu(z_ssm)).astype(bf16)

        merged = (_sigmoid(gates[:, 0:D_MODEL]) * _dot(y_pool.astype(bf16), wbp_ref[...])
                  + _sigmoid(gates[:, D_MODEL:2 * D_MODEL]) * _dot(y_ssm, wbs_ref[...]))
        out = _dot(merged.astype(bf16), wout_ref[...])
        rn = out * lax.rsqrt(jnp.mean(out * out, axis=-1, keepdims=True) + RMS_EPS) * npost_ref[...]
        o_ref[0, r0:r0 + SUB_T, :] = x + gate * rn
    ext_ref[0:POOL_HALO, :] = ext_ref[BLOCK_T:BLOCK_T + POOL_HALO, :]


def _layer(x, c, w_ada, b_ada, norm_pre, norm_post, w_in, pool_w, pool_scale, a_re, a_im, log_dt,
           b_re, b_im, c_re, c_im, d_skip, glu_w, glu_b, w_branch_pool, w_branch_ssm, w_out):
    bsz, seq, d = x.shape
    assert d == D_MODEL and seq % BLOCK_T == 0
    bf16 = jnp.bfloat16
    f32 = jnp.float32

    toep, bst, cstt, pow_tab = _ssm_prep(a_re, a_im, log_dt, b_re, b_im, c_re, c_im)
    c_pad = jnp.zeros((SUBLANES, d), f32).at[:bsz].set(c)
    mod = _adaln(c_pad, w_ada, b_ada)[:bsz].reshape(bsz, 3, d)
    w_in_b = w_in.astype(bf16)

    u_rows = _ssm_in(x, mod, norm_pre, w_in_b[:, 2 * d:3 * d])
    y_slabs = _ssm_core(u_rows, d_skip, toep, bst, cstt, pow_tab)

    row = lambda v: v.reshape(1, d)
    tok_spec = pl.BlockSpec((1, BLOCK_T, d), lambda b, j: (b, j, 0))
    slab_spec = pl.BlockSpec((1, N_SLABS, BLOCK_T, LANES), lambda b, j: (b, 0, j, 0))
    operands = [
        (x, tok_spec),
        (mod, pl.BlockSpec((1, 3, d), lambda b, j: (b, 0, 0))),
        (row(norm_pre), None), (row(norm_post), None),
        (w_in_b, None),
        (y_slabs, slab_spec),
        (pool_w.astype(bf16), None), (row(pool_scale), None),
        (glu_w.astype(bf16), None), (row(glu_b), None),
        (w_branch_pool.astype(bf16), None), (w_branch_ssm.astype(bf16), None),
        (w_out.astype(bf16), None),
    ]
    arrays = [a for a, _ in operands]
    specs = [s if s is not None else _const_spec(a.shape) for a, s in operands]

    return pl.pallas_call(
        _block_kernel,
        grid=(bsz, seq // BLOCK_T),
        in_specs=specs,
        out_specs=tok_spec,
        out_shape=jax.ShapeDtypeStruct((bsz, seq, d), x.dtype),
        scratch_shapes=[pltpu.VMEM((BLOCK_T + POOL_HALO, d), f32)],
        compiler_params=pltpu.CompilerParams(
            dimension_semantics=("arbitrary", "arbitrary"),
            vmem_limit_bytes=VMEM_LIMIT_BYTES),
        name="block",
    )(*arrays)


def kernel(x, c, w_ada, b_ada, norm_pre, norm_post, w_in, pool_w, pool_scale, ssm_a_re, ssm_a_im,
           ssm_log_dt, ssm_b_re, ssm_b_im, ssm_c_re, ssm_c_im, ssm_d, glu_w, glu_b, w_branch_pool,
           w_branch_ssm, w_out):
    for layer in range(w_in.shape[0]):
        x = _layer(x, c, w_ada[layer], b_ada[layer], norm_pre[layer], norm_post[layer],
                   w_in[layer], pool_w[layer], pool_scale[layer], ssm_a_re[layer],
                   ssm_a_im[layer], ssm_log_dt[layer], ssm_b_re[layer], ssm_b_im[layer],
                   ssm_c_re[layer], ssm_c_im[layer], ssm_d.reshape(ssm_d.shape[0], -1)[layer],
                   glu_w[layer], glu_b[layer], w_branch_pool[layer], w_branch_ssm[layer],
                   w_out[layer])
    return x
```

```python
import math

import numpy as np
import jax
import jax.numpy as jnp
from jax import lax
from jax.experimental import pallas as pl
from jax.experimental.pallas import tpu as pltpu

D_MODEL = 1024
POOL_WINDOWS = (2, 4, 8, 16)
POOL_GROUP_WIDTH = D_MODEL // len(POOL_WINDOWS)
POOL_HALO = 16
SSM_GROUP = 16
SSM_GROUPS = D_MODEL // SSM_GROUP
SSM_STATE = 64
RMS_EPS = 1e-6

SUBLANES = 8
LANES = 128
MXU_DIM = 256
CHUNK = 16
GROUPS_PER_SLAB = LANES // SSM_GROUP
N_SLABS = D_MODEL // LANES
PAIRS_PER_SLAB = GROUPS_PER_SLAB // 2
PAIR_STATES = 2 * SSM_STATE
SLAB_IN = CHUNK * LANES
PAIR_LANES = 2 * SSM_GROUP
PAIR_IN = CHUNK * PAIR_LANES
QUAD = LANES // PAIR_LANES
SLAB_STATES = PAIRS_PER_SLAB * 2 * PAIR_STATES
BLOCK_T = 512
SUB_T = 256
SSM_IN_BLOCK_T = 1024
SSM_BLOCK_T = 4096
SSM_BLOCK_CHUNKS = SSM_BLOCK_T // CHUNK
VMEM_LIMIT_BYTES = 56 * 1024 * 1024

_ROW_A1, _ROW_A2, _ROW_A4, _ROW_CARRY = 0, 1, 2, 8
_POW_ROWS = 16


def _sigmoid(v):
    return 0.5 * jnp.tanh(0.5 * v) + 0.5


def _silu(v):
    return v * _sigmoid(v)


def _gelu_tanh(v):
    c = math.sqrt(2.0 / math.pi)
    return 0.5 * v * (1.0 + jnp.tanh(c * (v + 0.044715 * (v * v * v))))


def _dot(a, b):
    return jnp.dot(a, b, preferred_element_type=jnp.float32)


def _dot_nt(a, b, precision=None):
    return lax.dot_general(a, b, (((1,), (1,)), ((), ())), precision=precision,
                           preferred_element_type=jnp.float32)


def _cmul_add(ar, ai, xr, xi, br, bi):
    return ar * xr - ai * xi + br, ar * xi + ai * xr + bi


def _rms_modulate(x, norm_gain, scale, shift):
    xn = x * lax.rsqrt(jnp.mean(x * x, axis=-1, keepdims=True) + RMS_EPS) * norm_gain
    return xn * (1.0 + scale) + shift


def _const_spec(shape):
    nd = len(shape)
    return pl.BlockSpec(shape, lambda *_: (0,) * nd, pipeline_mode=pl.Buffered(1))


def _ssm_prep_kernel(are_ref, aim_ref, ldt_ref, bre_ref, bim_ref, cre_ref, cim_ref,
                     toep_ref, bst_ref, cstt_ref, pow_ref):
    bf16 = jnp.bfloat16
    in_group = [lax.broadcasted_iota(jnp.int32, (SSM_GROUP, LANES), 1) // SSM_STATE == g
                for g in range(2)]
    lane_block = lax.broadcasted_iota(jnp.int32, (SSM_GROUP, LANES), 1) // SSM_GROUP

    def split(v_re, v_im, g):
        return jnp.concatenate([jnp.where(in_group[g], v_re, 0.0), jnp.where(in_group[g], v_im, 0.0)],
                               axis=1)

    pow_ref[...] = jnp.zeros_like(pow_ref)
    for m in range(PAIRS_PER_SLAB):
        lanes = slice(m * LANES, (m + 1) * LANES)
        dt = jnp.exp(ldt_ref[:, lanes])
        lam_re = jnp.minimum(are_ref[:, lanes], -1e-4)
        lam_im = aim_ref[:, lanes]
        l_re = lam_re * dt
        l_im = lam_im * dt

        def cmul(p, q):
            return p[0] * q[0] - p[1] * q[1], p[0] * q[1] + p[1] * q[0]

        mag = jnp.exp(l_re)
        pows = [(jnp.ones_like(l_re), jnp.zeros_like(l_re)), (mag * jnp.cos(l_im), mag * jnp.sin(l_im))]
        for _ in range(CHUNK - 1):
            pows.append(cmul(pows[-1], pows[1]))
        abar_re, abar_im = pows[1]
        den = lam_re * lam_re + lam_im * lam_im
        num_re = abar_re - 1.0
        f_re = (num_re * lam_re + abar_im * lam_im) / den
        f_im = (abar_im * lam_re - num_re * lam_im) / den
        b_re = bre_ref[:, lanes]
        b_im = bim_ref[:, lanes]
        bb_re = f_re * b_re - f_im * b_im
        bb_im = f_re * b_im + f_im * b_re
        c_re = cre_ref[:, lanes]
        c_im = cim_ref[:, lanes]

        for s in range(CHUNK):
            ar, ai = pows[CHUNK - 1 - s]
            w_re = ar * bb_re - ai * bb_im
            w_im = ar * bb_im + ai * bb_re
            for g in range(2):
                r0 = s * PAIR_LANES + g * SSM_GROUP
                bst_ref[0, m, r0:r0 + SSM_GROUP, :] = split(w_re, w_im, g).astype(bf16)

        z = []
        for n in range(CHUNK + 1):
            ar, ai = pows[n]
            z.append((ar * c_re - ai * c_im, -(ar * c_im + ai * c_re)))
        for t in range(CHUNK):
            for g in range(2):
                r0 = t * PAIR_LANES + g * SSM_GROUP
                cstt_ref[0, m, r0:r0 + SSM_GROUP, :] = split(z[t + 1][0], z[t + 1][1], g).astype(bf16)

        z_all = jnp.concatenate([jnp.concatenate([z[n][0], z[n][1]], axis=1) for n in range(CHUNK)],
                                axis=0)
        lane_pair = lax.broadcasted_iota(jnp.int32, (SSM_GROUP, PAIR_IN), 1)
        for g in range(2):
            lag = _dot_nt(split(bb_re, bb_im, g), z_all, precision=lax.Precision.HIGHEST)
            cols = []
            for q in range(CHUNK // QUAD):
                col = jnp.zeros((SSM_GROUP, LANES), jnp.float32)
                for i in range(QUAD):
                    n = q * QUAD + i
                    src = lag[:, (n // GROUPS_PER_SLAB) * LANES:(n // GROUPS_PER_SLAB + 1) * LANES]
                    dst_block = 2 * i + g
                    shift = (SSM_GROUP * (dst_block - n % GROUPS_PER_SLAB)) % LANES
                    moved = pltpu.roll(src, shift, 1) if shift else src
                    col = jnp.where(lane_block == dst_block, moved, col)
                cols.append(col)
            lag_pair = jnp.concatenate(cols, axis=1)
            for s in range(CHUNK):
                r0 = s * PAIR_LANES + g * SSM_GROUP
                if s == 0:
                    blk = lag_pair
                else:
                    blk = jnp.where(lane_pair >= s * PAIR_LANES,
                                    pltpu.roll(lag_pair, s * PAIR_LANES, 1), 0.0)
                toep_ref[0, m, r0:r0 + SSM_GROUP, :] = blk.astype(bf16)

        def pow_row(p):
            return jnp.concatenate([p[0], p[1]], axis=1)

        cols = slice(m * 2 * PAIR_STATES, (m + 1) * 2 * PAIR_STATES)
        a16 = pows[CHUNK]
        a32 = cmul(a16, a16)
        pow_ref[_ROW_A1:_ROW_A1 + 1, cols] = pow_row(a16)
        pow_ref[_ROW_A2:_ROW_A2 + 1, cols] = pow_row(a32)
        pow_ref[_ROW_A4:_ROW_A4 + 1, cols] = pow_row(cmul(a32, a32))
        carry_pow = a16
        for r in range(SUBLANES):
            pow_ref[_ROW_CARRY + r:_ROW_CARRY + r + 1, cols] = pow_row(carry_pow)
            carry_pow = cmul(carry_pow, a16)


def _ssm_prep(a_re, a_im, log_dt, b_re, b_im, c_re, c_im):
    g, p, h = SSM_GROUPS, SSM_STATE, SSM_GROUP
    slab_lanes = GROUPS_PER_SLAB * p
    flat = lambda v: v.reshape(1, g * p)
    row_spec = pl.BlockSpec((1, slab_lanes), lambda v: (0, v))
    mat_spec = pl.BlockSpec((h, slab_lanes), lambda v: (0, v))
    bf16 = jnp.bfloat16
    states = (N_SLABS, PAIRS_PER_SLAB, PAIR_IN, 2 * PAIR_STATES)
    toep = (N_SLABS, PAIRS_PER_SLAB, PAIR_IN, PAIR_IN)
    return pl.pallas_call(
        _ssm_prep_kernel,
        grid=(N_SLABS,),
        in_specs=[row_spec, row_spec, row_spec, mat_spec, mat_spec, mat_spec, mat_spec],
        out_specs=(pl.BlockSpec((1,) + toep[1:], lambda v: (v, 0, 0, 0)),
                   pl.BlockSpec((1,) + states[1:], lambda v: (v, 0, 0, 0)),
                   pl.BlockSpec((1,) + states[1:], lambda v: (v, 0, 0, 0)),
                   pl.BlockSpec((_POW_ROWS, SLAB_STATES), lambda v: (0, v))),
        out_shape=(jax.ShapeDtypeStruct(toep, bf16),
                   jax.ShapeDtypeStruct(states, bf16),
                   jax.ShapeDtypeStruct(states, bf16),
                   jax.ShapeDtypeStruct((_POW_ROWS, N_SLABS * SLAB_STATES), jnp.float32)),
        name="ssm_prep",
    )(flat(a_re), flat(a_im), flat(jnp.repeat(log_dt, p)),
      b_re.reshape(g * p, h).T, b_im.reshape(g * p, h).T,
      c_re.transpose(1, 0, 2).reshape(h, g * p), c_im.transpose(1, 0, 2).reshape(h, g * p))


def _adaln_kernel(c_ref, w_ref, b_ref, o_ref):
    o_ref[...] = _dot(_silu(c_ref[...]), w_ref[...]) + b_ref[...]


def _adaln(c_pad, w_ada, b_ada):
    rows = c_pad.shape[0]
    return pl.pallas_call(
        _adaln_kernel,
        grid=(3,),
        in_specs=[pl.BlockSpec((rows, D_MODEL), lambda j: (0, 0)),
                  pl.BlockSpec((D_MODEL, D_MODEL), lambda j: (0, j)),
                  pl.BlockSpec((1, D_MODEL), lambda j: (0, j))],
        out_specs=pl.BlockSpec((rows, D_MODEL), lambda j: (0, j)),
        out_shape=jax.ShapeDtypeStruct((rows, 3 * D_MODEL), jnp.float32),
        name="adaln",
    )(c_pad, w_ada, b_ada.reshape(1, 3 * D_MODEL))


def _ssm_in_kernel(x_ref, mod_ref, npre_ref, perm_ref, wu_ref, u_ref):
    sub_chunks = SUB_T // CHUNK
    for sub in range(SSM_IN_BLOCK_T // SUB_T):
        r0 = sub * SUB_T
        hb = _rms_modulate(x_ref[0, r0:r0 + SUB_T, :], npre_ref[...], mod_ref[0, 1:2, :],
                           mod_ref[0, 0:1, :]).astype(jnp.bfloat16)
        hp = _dot(perm_ref[...], hb).astype(jnp.bfloat16)
        u = _dot(hp, wu_ref[...])
        for v in range(N_SLABS):
            for t in range(CHUNK):
                u_ref[0, v, sub * sub_chunks:(sub + 1) * sub_chunks, t * LANES:(t + 1) * LANES] = (
                    u[t * sub_chunks:(t + 1) * sub_chunks, v * LANES:(v + 1) * LANES])


def _chunk_permutation():
    n_chunks = SUB_T // CHUNK
    p = np.zeros((SUB_T, SUB_T), np.float32)
    for c in range(n_chunks):
        for t in range(CHUNK):
            p[t * n_chunks + c, c * CHUNK + t] = 1.0
    return p


def _ssm_in(x, mod, norm_pre, w_u):
    bsz, seq, d = x.shape
    return pl.pallas_call(
        _ssm_in_kernel,
        grid=(bsz, seq // SSM_IN_BLOCK_T),
        in_specs=[pl.BlockSpec((1, SSM_IN_BLOCK_T, d), lambda b, j: (b, j, 0)),
                  pl.BlockSpec((1, 3, d), lambda b, j: (b, 0, 0)),
                  _const_spec((1, d)), _const_spec((SUB_T, SUB_T)), _const_spec((d, d))],
        out_specs=pl.BlockSpec((1, N_SLABS, SSM_IN_BLOCK_T // CHUNK, SLAB_IN), lambda b, j: (b, 0, j, 0)),
        out_shape=jax.ShapeDtypeStruct((bsz, N_SLABS, seq // CHUNK, SLAB_IN), jnp.float32),
        compiler_params=pltpu.CompilerParams(dimension_semantics=("arbitrary", "arbitrary"),
                                             vmem_limit_bytes=VMEM_LIMIT_BYTES),
        name="ssm_in",
    )(x, mod, norm_pre.reshape(1, d), jnp.asarray(_chunk_permutation(), jnp.bfloat16), w_u)


def _swap_stage(tiles, d):
    block = lax.broadcasted_iota(jnp.int32, tiles[0].shape, 1) // PAIR_LANES
    keep = (block & d) == 0
    out = list(tiles)
    for i in range(PAIRS_PER_SLAB):
        if i & d:
            continue
        a, b = tiles[i], tiles[i + d]
        out[i] = jnp.where(keep, a, pltpu.roll(b, d * PAIR_LANES, 1))
        out[i + d] = jnp.where(keep, pltpu.roll(a, LANES - d * PAIR_LANES, 1), b)
    return out


def _ssm_core_kernel(u_ref, toep_ref, bst_ref, cstt_ref, pow_ref, y_ref,
                     stage_in_ref, up_ref, s_ref, xprev_ref, yin_ref, carry_ref):
    bf16 = jnp.bfloat16
    n_quads = CHUNK // QUAD
    tile = lambda k: slice(k * LANES, (k + 1) * LANES)
    rows16 = 2 * SUBLANES

    @pl.when(pl.program_id(2) == 0)
    def _():
        carry_ref[...] = jnp.zeros_like(carry_ref)

    for c0 in range(0, SSM_BLOCK_CHUNKS, rows16):
        rows = slice(c0, c0 + rows16)
        for q in range(n_quads):
            out = _swap_stage([u_ref[0, 0, rows, tile(q * QUAD + i)].astype(bf16) for i in range(QUAD)], 2)
            for i in range(QUAD):
                stage_in_ref[rows, tile(q * QUAD + i)] = out[i]
    for c0 in range(0, SSM_BLOCK_CHUNKS, rows16):
        rows = slice(c0, c0 + rows16)
        for q in range(n_quads):
            out = _swap_stage([stage_in_ref[rows, tile(q * QUAD + i)] for i in range(QUAD)], 1)
            for m in range(PAIRS_PER_SLAB):
                up_ref[m, rows, tile(q)] = out[m]

    for m in range(PAIRS_PER_SLAB):
        up = up_ref[m]
        s_ref[:, m * 2 * PAIR_STATES:(m + 1) * 2 * PAIR_STATES] = _dot(up, bst_ref[0, m])
        for nt in range(PAIR_IN // MXU_DIM):
            cols = slice(nt * MXU_DIM, (nt + 1) * MXU_DIM)
            k_hi = (nt + 1) * MXU_DIM
            yin_ref[m, :, cols] = _dot(up[:, 0:k_hi], toep_ref[0, m, 0:k_hi, cols])

    row = lax.broadcasted_iota(jnp.int32, (SUBLANES, PAIR_STATES), 0)
    for m in range(PAIRS_PER_SLAB):
        re = slice(m * 2 * PAIR_STATES, m * 2 * PAIR_STATES + PAIR_STATES)
        im = slice(m * 2 * PAIR_STATES + PAIR_STATES, (m + 1) * 2 * PAIR_STATES)
        bcast = lambda r, lanes: jnp.broadcast_to(pow_ref[r:r + 1, lanes], (SUBLANES, PAIR_STATES))
        steps = [(sh, bcast(r, re), bcast(r, im)) for sh, r in ((1, _ROW_A1), (2, _ROW_A2), (4, _ROW_A4))]
        pc_r = pow_ref[_ROW_CARRY:_ROW_CARRY + SUBLANES, re]
        pc_i = pow_ref[_ROW_CARRY:_ROW_CARRY + SUBLANES, im]
        cr = carry_ref[m, 0]
        ci = carry_ref[m, 1]
        for c0 in range(0, SSM_BLOCK_CHUNKS, rows16):
            prev = []
            for o in range(2):
                rows = slice(c0 + o * SUBLANES, c0 + (o + 1) * SUBLANES)
                er = s_ref[rows, re]
                ei = s_ref[rows, im]
                for sh, ar, ai in steps:
                    tr = jnp.where(row >= sh, pltpu.roll(er, sh, 0), 0.0)
                    ti = jnp.where(row >= sh, pltpu.roll(ei, sh, 0), 0.0)
                    er, ei = _cmul_add(ar, ai, tr, ti, er, ei)
                er, ei = _cmul_add(pc_r, pc_i, cr, ci, er, ei)
                prev.append((jnp.where(row == 0, cr, pltpu.roll(er, 1, 0)),
                             jnp.where(row == 0, ci, pltpu.roll(ei, 1, 0))))
                cr = jnp.broadcast_to(er[SUBLANES - 1:SUBLANES], (SUBLANES, PAIR_STATES))
                ci = jnp.broadcast_to(ei[SUBLANES - 1:SUBLANES], (SUBLANES, PAIR_STATES))
            rows = slice(c0, c0 + rows16)
            xprev_ref[rows, re] = jnp.concatenate([prev[0][0], prev[1][0]], axis=0).astype(bf16)
            xprev_ref[rows, im] = jnp.concatenate([prev[0][1], prev[1][1]], axis=0).astype(bf16)
        carry_ref[m, 0] = cr
        carry_ref[m, 1] = ci

    for m in range(PAIRS_PER_SLAB):
        y_ref[0, m] = yin_ref[m] + _dot_nt(xprev_ref[:, m * 2 * PAIR_STATES:(m + 1) * 2 * PAIR_STATES],
                                           cstt_ref[0, m])


def _ssm_core(u_rows, toep, bst, cstt, pow_tab):
    bsz, n_slabs, n_chunks, lanes = u_rows.shape
    seq = n_chunks * CHUNK
    assert seq % SSM_BLOCK_T == 0 and lanes == SLAB_IN
    bf16 = jnp.bfloat16
    f32 = jnp.float32
    tok_spec = pl.BlockSpec((1, 1, SSM_BLOCK_CHUNKS, SLAB_IN), lambda v, b, j: (b, v, j, 0))
    return pl.pallas_call(
        _ssm_core_kernel,
        grid=(N_SLABS, bsz, seq // SSM_BLOCK_T),
        in_specs=[tok_spec,
                  pl.BlockSpec((1,) + toep.shape[1:], lambda v, b, j: (v, 0, 0, 0)),
                  pl.BlockSpec((1,) + bst.shape[1:], lambda v, b, j: (v, 0, 0, 0)),
                  pl.BlockSpec((1,) + cstt.shape[1:], lambda v, b, j: (v, 0, 0, 0)),
                  pl.BlockSpec((_POW_ROWS, SLAB_STATES), lambda v, b, j: (0, v))],
        out_specs=pl.BlockSpec((1, PAIRS_PER_SLAB, SSM_BLOCK_CHUNKS, PAIR_IN), lambda v, b, j: (b, v, j, 0)),
        out_shape=jax.ShapeDtypeStruct((bsz, n_slabs * PAIRS_PER_SLAB, n_chunks, PAIR_IN), f32),
        scratch_shapes=[
            pltpu.VMEM((SSM_BLOCK_CHUNKS, SLAB_IN), bf16),
            pltpu.VMEM((PAIRS_PER_SLAB, SSM_BLOCK_CHUNKS, PAIR_IN), bf16),
            pltpu.VMEM((SSM_BLOCK_CHUNKS, SLAB_STATES), f32),
            pltpu.VMEM((SSM_BLOCK_CHUNKS, SLAB_STATES), bf16),
            pltpu.VMEM((PAIRS_PER_SLAB, SSM_BLOCK_CHUNKS, PAIR_IN), f32),
            pltpu.VMEM((PAIRS_PER_SLAB, 2, SUBLANES, PAIR_STATES), f32),
        ],
        compiler_params=pltpu.CompilerParams(
            dimension_semantics=("arbitrary", "arbitrary", "arbitrary"),
            vmem_limit_bytes=VMEM_LIMIT_BYTES),
        name="ssm_core",
    )(u_rows, toep, bst, cstt, pow_tab)


def _block_kernel(x_ref, mod_ref, npre_ref, npost_ref, win_ref, u_ref, yp_ref, dskip_ref, poolw_ref,
                  pscale_ref, gluw_ref, glub_ref, wbp_ref, wbs_ref, wout_ref, o_ref,
                  ext_ref, stage_ref, ytok_ref):
    j = pl.program_id(1)
    bf16 = jnp.bfloat16

    @pl.when(j == 0)
    def _():
        ext_ref[0:POOL_HALO, :] = jnp.zeros((POOL_HALO, D_MODEL), jnp.float32)

    gate = mod_ref[0, 2:3, :]
    for sub in range(BLOCK_T // SUB_T):
        r0 = sub * SUB_T
        x = x_ref[0, r0:r0 + SUB_T, :]
        hb = _rms_modulate(x, npre_ref[...], mod_ref[0, 1:2, :], mod_ref[0, 0:1, :]).astype(bf16)

        pn = _dot(hb, win_ref[:, 0:2 * D_MODEL])
        z_ssm = _dot(hb, win_ref[:, 3 * D_MODEL:4 * D_MODEL])
        gates = _dot(hb, win_ref[:, 4 * D_MODEL:6 * D_MODEL])
        u_pool = pn[:, 0:D_MODEL]

        e0 = POOL_HALO + r0
        ext_ref[e0:e0 + SUB_T, :] = u_pool
        pos = (j * BLOCK_T + r0 + 1
               + lax.broadcasted_iota(jnp.int32, (SUB_T, 1), 0)).astype(jnp.float32)
        mixed = []
        for g, w in enumerate(POOL_WINDOWS):
            cols = slice(g * POOL_GROUP_WIDTH, (g + 1) * POOL_GROUP_WIDTH)
            wsum = u_pool[:, cols]
            for lag in range(1, w):
                wsum = wsum + ext_ref[e0 - lag:e0 - lag + SUB_T, cols]
            pooled = wsum / jnp.minimum(pos, float(w)) - u_pool[:, cols]
            mixed.append(_dot(pooled.astype(bf16), poolw_ref[g]))
        y_pool = (jnp.concatenate(mixed, axis=-1) * pscale_ref[...]
                  * _silu(pn[:, D_MODEL:2 * D_MODEL]))

        chunk_rows = range(r0 // CHUNK, (r0 + SUB_T) // CHUNK, SUBLANES)
        tile = lambda k: slice(k * LANES, (k + 1) * LANES)
        for v in range(N_SLABS):
            pairs = [v * PAIRS_PER_SLAB + m for m in range(PAIRS_PER_SLAB)]
            for c0 in chunk_rows:
                rows = slice(c0, c0 + SUBLANES)
                for q in range(CHUNK // QUAD):
                    out = _swap_stage([yp_ref[0, p, rows, tile(q)] for p in pairs], 2)
                    for p, o in zip(pairs, out):
                        stage_ref[p, rows, tile(q)] = o
        for v in range(N_SLABS):
            pairs = [v * PAIRS_PER_SLAB + m for m in range(PAIRS_PER_SLAB)]
            for c0 in chunk_rows:
                rows = slice(c0, c0 + SUBLANES)
                for q in range(CHUNK // QUAD):
                    out = _swap_stage([stage_ref[p, rows, tile(q)] for p in pairs], 1)
                    for i in range(QUAD):
                        t = q * QUAD + i
                        ytok_ref[v, pl.ds(c0 * CHUNK + t, SUBLANES, stride=CHUNK), :] = (
                            out[i] + dskip_ref[v, :, tile(t)] * u_ref[0, v, rows, tile(t)])

        y = _gelu_tanh(jnp.concatenate([ytok_ref[v, r0:r0 + SUB_T, :] for v in range(N_SLABS)],
                                       axis=-1))
        y = y * _sigmoid(_dot(y.astype(bf16), gluw_ref[...]) + glub_ref[...])
        y_ssm = (y * _silu(z_ssm)).astype(bf16)

        merged = (_sigmoid(gates[:, 0:D_MODEL]) * _dot(y_pool.astype(bf16), wbp_ref[...])
                  + _sigmoid(gates[:, D_MODEL:2 * D_MODEL]) * _dot(y_ssm, wbs_ref[...]))
        out = _dot(merged.astype(bf16), wout_ref[...])
        rn = out * lax.rsqrt(jnp.mean(out * out, axis=-1, keepdims=True) + RMS_EPS) * npost_ref[...]
        o_ref[0, r0:r0 + SUB_T, :] = x + gate * rn
    ext_ref[0:POOL_HALO, :] = ext_ref[BLOCK_T:BLOCK_T + POOL_HALO, :]


def _layer(x, c, w_ada, b_ada, norm_pre, norm_post, w_in, pool_w, pool_scale, a_re, a_im, log_dt,
           b_re, b_im, c_re, c_im, d_skip, glu_w, glu_b, w_branch_pool, w_branch_ssm, w_out):
    bsz, seq, d = x.shape
    assert d == D_MODEL and seq % BLOCK_T == 0
    bf16 = jnp.bfloat16
    f32 = jnp.float32

    toep, bst, cstt, pow_tab = _ssm_prep(a_re, a_im, log_dt, b_re, b_im, c_re, c_im)
    c_pad = jnp.zeros((SUBLANES, d), f32).at[:bsz].set(c)
    mod = _adaln(c_pad, w_ada, b_ada)[:bsz].reshape(bsz, 3, d)
    w_in_b = w_in.astype(bf16)

    u_rows = _ssm_in(x, mod, norm_pre, w_in_b[:, 2 * d:3 * d])
    y_pairs = _ssm_core(u_rows, toep, bst, cstt, pow_tab)
    d_rows = jnp.tile(d_skip.reshape(N_SLABS, 1, LANES), (1, 1, CHUNK))

    row = lambda v: v.reshape(1, d)
    tok_spec = pl.BlockSpec((1, BLOCK_T, d), lambda b, j: (b, j, 0))
    block_chunks = BLOCK_T // CHUNK
    u_spec = pl.BlockSpec((1, N_SLABS, block_chunks, SLAB_IN), lambda b, j: (b, 0, j, 0))
    y_spec = pl.BlockSpec((1, N_SLABS * PAIRS_PER_SLAB, block_chunks, PAIR_IN), lambda b, j: (b, 0, j, 0))
    operands = [
        (x, tok_spec),
        (mod, pl.BlockSpec((1, 3, d), lambda b, j: (b, 0, 0))),
        (row(norm_pre), None), (row(norm_post), None),
        (w_in_b, None),
        (u_rows, u_spec), (y_pairs, y_spec), (d_rows, None),
        (pool_w.astype(bf16), None), (row(pool_scale), None),
        (glu_w.astype(bf16), None), (row(glu_b), None),
        (w_branch_pool.astype(bf16), None), (w_branch_ssm.astype(bf16), None),
        (w_out.astype(bf16), None),
    ]
    arrays = [a for a, _ in operands]
    specs = [s if s is not None else _const_spec(a.shape) for a, s in operands]

    return pl.pallas_call(
        _block_kernel,
        grid=(bsz, seq // BLOCK_T),
        in_specs=specs,
        out_specs=tok_spec,
        out_shape=jax.ShapeDtypeStruct((bsz, seq, d), x.dtype),
        scratch_shapes=[
            pltpu.VMEM((BLOCK_T + POOL_HALO, d), f32),
            pltpu.VMEM((N_SLABS * PAIRS_PER_SLAB, BLOCK_T // CHUNK, PAIR_IN), f32),
            pltpu.VMEM((N_SLABS, BLOCK_T, LANES), f32),
        ],
        compiler_params=pltpu.CompilerParams(
            dimension_semantics=("arbitrary", "arbitrary"),
            vmem_limit_bytes=VMEM_LIMIT_BYTES),
        name="block",
    )(*arrays)


def kernel(x, c, w_ada, b_ada, norm_pre, norm_post, w_in, pool_w, pool_scale, ssm_a_re, ssm_a_im,
           ssm_log_dt, ssm_b_re, ssm_b_im, ssm_c_re, ssm_c_im, ssm_d, glu_w, glu_b, w_branch_pool,
           w_branch_ssm, w_out):
    for layer in range(w_in.shape[0]):
        x = _layer(x, c, w_ada[layer], b_ada[layer], norm_pre[layer], norm_post[layer],
                   w_in[layer], pool_w[layer], pool_scale[layer], ssm_a_re[layer],
                   ssm_a_im[layer], ssm_log_dt[layer], ssm_b_re[layer], ssm_b_im[layer],
                   ssm_c_re[layer], ssm_c_im[layer], ssm_d.reshape(ssm_d.shape[0], -1)[layer],
                   glu_w[layer], glu_b[layer], w_branch_pool[layer], w_branch_ssm[layer],
                   w_out[layer])
    return x
```

```python
import math

import numpy as np
import jax
import jax.numpy as jnp
from jax import lax
from jax.experimental import pallas as pl
from jax.experimental.pallas import tpu as pltpu

D_MODEL = 1024
POOL_WINDOWS = (2, 4, 8, 16)
POOL_GROUP_WIDTH = D_MODEL // len(POOL_WINDOWS)
POOL_HALO = 16
SSM_GROUP = 16
SSM_GROUPS = D_MODEL // SSM_GROUP
SSM_STATE = 64
RMS_EPS = 1e-6

SUBLANES = 8
LANES = 128
MXU_DIM = 256
CHUNK = 16
GROUPS_PER_SLAB = LANES // SSM_GROUP
N_SLABS = D_MODEL // LANES
PAIRS_PER_SLAB = GROUPS_PER_SLAB // 2
PAIR_STATES = 2 * SSM_STATE
SLAB_IN = CHUNK * LANES
PAIR_LANES = 2 * SSM_GROUP
PAIR_IN = CHUNK * PAIR_LANES
QUAD = LANES // PAIR_LANES
SLAB_STATES = PAIRS_PER_SLAB * 2 * PAIR_STATES
BLOCK_T = 512
SUB_T = 256
SSM_IN_BLOCK_T = 1024
SSM_BLOCK_T = 4096
SSM_BLOCK_CHUNKS = SSM_BLOCK_T // CHUNK
VMEM_LIMIT_BYTES = 56 * 1024 * 1024

_ROW_A1, _ROW_A2, _ROW_A4, _ROW_CARRY = 0, 1, 2, 8
_POW_ROWS = 16


def _sigmoid(v):
    return 0.5 * jnp.tanh(0.5 * v) + 0.5


def _silu(v):
    return v * _sigmoid(v)


def _gelu_tanh(v):
    c = math.sqrt(2.0 / math.pi)
    return 0.5 * v * (1.0 + jnp.tanh(c * (v + 0.044715 * (v * v * v))))


def _dot(a, b):
    return jnp.dot(a, b, preferred_element_type=jnp.float32)


def _dot_nt(a, b, precision=None):
    return lax.dot_general(a, b, (((1,), (1,)), ((), ())), precision=precision,
                           preferred_element_type=jnp.float32)


def _cmul_add(ar, ai, xr, xi, br, bi):
    return ar * xr - ai * xi + br, ar * xi + ai * xr + bi


def _rms_modulate(x, norm_gain, scale, shift):
    xn = x * lax.rsqrt(jnp.mean(x * x, axis=-1, keepdims=True) + RMS_EPS) * norm_gain
    return xn * (1.0 + scale) + shift


def _const_spec(shape):
    nd = len(shape)
    return pl.BlockSpec(shape, lambda *_: (0,) * nd, pipeline_mode=pl.Buffered(1))


def _ssm_prep_kernel(are_ref, aim_ref, ldt_ref, bre_ref, bim_ref, cre_ref, cim_ref,
                     toep_ref, bst_ref, cstt_ref, pow_ref):
    bf16 = jnp.bfloat16
    in_group = [lax.broadcasted_iota(jnp.int32, (SSM_GROUP, LANES), 1) // SSM_STATE == g
                for g in range(2)]
    lane_block = lax.broadcasted_iota(jnp.int32, (SSM_GROUP, LANES), 1) // SSM_GROUP

    def split(v_re, v_im, g):
        return jnp.concatenate([jnp.where(in_group[g], v_re, 0.0), jnp.where(in_group[g], v_im, 0.0)],
                               axis=1)

    pow_ref[...] = jnp.zeros_like(pow_ref)
    for m in range(PAIRS_PER_SLAB):
        lanes = slice(m * LANES, (m + 1) * LANES)
        dt = jnp.exp(ldt_ref[:, lanes])
        lam_re = jnp.minimum(are_ref[:, lanes], -1e-4)
        lam_im = aim_ref[:, lanes]
        l_re = lam_re * dt
        l_im = lam_im * dt

        def cmul(p, q):
            return p[0] * q[0] - p[1] * q[1], p[0] * q[1] + p[1] * q[0]

        mag = jnp.exp(l_re)
        pows = [(jnp.ones_like(l_re), jnp.zeros_like(l_re)), (mag * jnp.cos(l_im), mag * jnp.sin(l_im))]
        for _ in range(CHUNK - 1):
            pows.append(cmul(pows[-1], pows[1]))
        abar_re, abar_im = pows[1]
        den = lam_re * lam_re + lam_im * lam_im
        num_re = abar_re - 1.0
        f_re = (num_re * lam_re + abar_im * lam_im) / den
        f_im = (abar_im * lam_re - num_re * lam_im) / den
        b_re = bre_ref[:, lanes]
        b_im = bim_ref[:, lanes]
        bb_re = f_re * b_re - f_im * b_im
        bb_im = f_re * b_im + f_im * b_re
        c_re = cre_ref[:, lanes]
        c_im = cim_ref[:, lanes]

        for s in range(CHUNK):
            ar, ai = pows[CHUNK - 1 - s]
            w_re = ar * bb_re - ai * bb_im
            w_im = ar * bb_im + ai * bb_re
            for g in range(2):
                r0 = s * PAIR_LANES + g * SSM_GROUP
                bst_ref[0, m, r0:r0 + SSM_GROUP, :] = split(w_re, w_im, g).astype(bf16)

        z = []
        for n in range(CHUNK + 1):
            ar, ai = pows[n]
            z.append((ar * c_re - ai * c_im, -(ar * c_im + ai * c_re)))
        for t in range(CHUNK):
            for g in range(2):
                r0 = t * PAIR_LANES + g * SSM_GROUP
                cstt_ref[0, m, r0:r0 + SSM_GROUP, :] = split(z[t + 1][0], z[t + 1][1], g).astype(bf16)

        z_all = jnp.concatenate([jnp.concatenate([z[n][0], z[n][1]], axis=1) for n in range(CHUNK)],
                                axis=0)
        lane_pair = lax.broadcasted_iota(jnp.int32, (SSM_GROUP, PAIR_IN), 1)
        for g in range(2):
            lag = _dot_nt(split(bb_re, bb_im, g), z_all, precision=lax.Precision.HIGHEST)
            cols = []
            for q in range(CHUNK // QUAD):
                col = jnp.zeros((SSM_GROUP, LANES), jnp.float32)
                for i in range(QUAD):
                    n = q * QUAD + i
                    src = lag[:, (n // GROUPS_PER_SLAB) * LANES:(n // GROUPS_PER_SLAB + 1) * LANES]
                    dst_block = 2 * i + g
                    shift = (SSM_GROUP * (dst_block - n % GROUPS_PER_SLAB)) % LANES
                    moved = pltpu.roll(src, shift, 1) if shift else src
                    col = jnp.where(lane_block == dst_block, moved, col)
                cols.append(col)
            lag_pair = jnp.concatenate(cols, axis=1)
            for s in range(CHUNK):
                r0 = s * PAIR_LANES + g * SSM_GROUP
                if s == 0:
                    blk = lag_pair
                else:
                    blk = jnp.where(lane_pair >= s * PAIR_LANES,
                                    pltpu.roll(lag_pair, s * PAIR_LANES, 1), 0.0)
                toep_ref[0, m, r0:r0 + SSM_GROUP, :] = blk.astype(bf16)

        def pow_row(p):
            return jnp.concatenate([p[0], p[1]], axis=1)

        cols = slice(m * 2 * PAIR_STATES, (m + 1) * 2 * PAIR_STATES)
        a16 = pows[CHUNK]
        a32 = cmul(a16, a16)
        pow_ref[_ROW_A1:_ROW_A1 + 1, cols] = pow_row(a16)
        pow_ref[_ROW_A2:_ROW_A2 + 1, cols] = pow_row(a32)
        pow_ref[_ROW_A4:_ROW_A4 + 1, cols] = pow_row(cmul(a32, a32))
        carry_pow = a16
        for r in range(SUBLANES):
            pow_ref[_ROW_CARRY + r:_ROW_CARRY + r + 1, cols] = pow_row(carry_pow)
            carry_pow = cmul(carry_pow, a16)


def _ssm_prep(a_re, a_im, log_dt, b_re, b_im, c_re, c_im):
    g, p, h = SSM_GROUPS, SSM_STATE, SSM_GROUP
    slab_lanes = GROUPS_PER_SLAB * p
    flat = lambda v: v.reshape(1, g * p)
    row_spec = pl.BlockSpec((1, slab_lanes), lambda v: (0, v))
    mat_spec = pl.BlockSpec((h, slab_lanes), lambda v: (0, v))
    bf16 = jnp.bfloat16
    states = (N_SLABS, PAIRS_PER_SLAB, PAIR_IN, 2 * PAIR_STATES)
    toep = (N_SLABS, PAIRS_PER_SLAB, PAIR_IN, PAIR_IN)
    return pl.pallas_call(
        _ssm_prep_kernel,
        grid=(N_SLABS,),
        in_specs=[row_spec, row_spec, row_spec, mat_spec, mat_spec, mat_spec, mat_spec],
        out_specs=(pl.BlockSpec((1,) + toep[1:], lambda v: (v, 0, 0, 0)),
                   pl.BlockSpec((1,) + states[1:], lambda v: (v, 0, 0, 0)),
                   pl.BlockSpec((1,) + states[1:], lambda v: (v, 0, 0, 0)),
                   pl.BlockSpec((_POW_ROWS, SLAB_STATES), lambda v: (0, v))),
        out_shape=(jax.ShapeDtypeStruct(toep, bf16),
                   jax.ShapeDtypeStruct(states, bf16),
                   jax.ShapeDtypeStruct(states, bf16),
                   jax.ShapeDtypeStruct((_POW_ROWS, N_SLABS * SLAB_STATES), jnp.float32)),
        name="ssm_prep",
    )(flat(a_re), flat(a_im), flat(jnp.repeat(log_dt, p)),
      b_re.reshape(g * p, h).T, b_im.reshape(g * p, h).T,
      c_re.transpose(1, 0, 2).reshape(h, g * p), c_im.transpose(1, 0, 2).reshape(h, g * p))


def _adaln_kernel(c_ref, w_ref, b_ref, o_ref):
    o_ref[...] = _dot(_silu(c_ref[...]), w_ref[...]) + b_ref[...]


def _adaln(c_pad, w_ada, b_ada):
    rows = c_pad.shape[0]
    return pl.pallas_call(
        _adaln_kernel,
        grid=(3,),
        in_specs=[pl.BlockSpec((rows, D_MODEL), lambda j: (0, 0)),
                  pl.BlockSpec((D_MODEL, D_MODEL), lambda j: (0, j)),
                  pl.BlockSpec((1, D_MODEL), lambda j: (0, j))],
        out_specs=pl.BlockSpec((rows, D_MODEL), lambda j: (0, j)),
        out_shape=jax.ShapeDtypeStruct((rows, 3 * D_MODEL), jnp.float32),
        name="adaln",
    )(c_pad, w_ada, b_ada.reshape(1, 3 * D_MODEL))


def _ssm_in_kernel(x_ref, mod_ref, npre_ref, perm_ref, wu_ref, u_ref, ub_ref):
    sub_chunks = SUB_T // CHUNK
    for sub in range(SSM_IN_BLOCK_T // SUB_T):
        r0 = sub * SUB_T
        hb = _rms_modulate(x_ref[0, r0:r0 + SUB_T, :], npre_ref[...], mod_ref[0, 1:2, :],
                           mod_ref[0, 0:1, :]).astype(jnp.bfloat16)
        hp = _dot(perm_ref[...], hb).astype(jnp.bfloat16)
        u = _dot(hp, wu_ref[...])
        for v in range(N_SLABS):
            for t in range(CHUNK):
                piece = u[t * sub_chunks:(t + 1) * sub_chunks, v * LANES:(v + 1) * LANES]
                dst = (0, v, slice(sub * sub_chunks, (sub + 1) * sub_chunks), slice(t * LANES, (t + 1) * LANES))
                u_ref[dst] = piece
                ub_ref[dst] = piece.astype(jnp.bfloat16)


def _chunk_permutation():
    n_chunks = SUB_T // CHUNK
    p = np.zeros((SUB_T, SUB_T), np.float32)
    for c in range(n_chunks):
        for t in range(CHUNK):
            p[t * n_chunks + c, c * CHUNK + t] = 1.0
    return p


def _ssm_in(x, mod, norm_pre, w_u):
    bsz, seq, d = x.shape
    rows_spec = pl.BlockSpec((1, N_SLABS, SSM_IN_BLOCK_T // CHUNK, SLAB_IN), lambda b, j: (b, 0, j, 0))
    rows_shape = (bsz, N_SLABS, seq // CHUNK, SLAB_IN)
    return pl.pallas_call(
        _ssm_in_kernel,
        grid=(bsz, seq // SSM_IN_BLOCK_T),
        in_specs=[pl.BlockSpec((1, SSM_IN_BLOCK_T, d), lambda b, j: (b, j, 0)),
                  pl.BlockSpec((1, 3, d), lambda b, j: (b, 0, 0)),
                  _const_spec((1, d)), _const_spec((SUB_T, SUB_T)), _const_spec((d, d))],
        out_specs=(rows_spec, rows_spec),
        out_shape=(jax.ShapeDtypeStruct(rows_shape, jnp.float32),
                   jax.ShapeDtypeStruct(rows_shape, jnp.bfloat16)),
        compiler_params=pltpu.CompilerParams(dimension_semantics=("arbitrary", "arbitrary"),
                                             vmem_limit_bytes=VMEM_LIMIT_BYTES),
        name="ssm_in",
    )(x, mod, norm_pre.reshape(1, d), jnp.asarray(_chunk_permutation(), jnp.bfloat16), w_u)


def _swap_stage(tiles, d):
    block = lax.broadcasted_iota(jnp.int32, tiles[0].shape, 1) // PAIR_LANES
    keep = (block & d) == 0
    out = list(tiles)
    for i in range(PAIRS_PER_SLAB):
        if i & d:
            continue
        a, b = tiles[i], tiles[i + d]
        out[i] = jnp.where(keep, a, pltpu.roll(b, d * PAIR_LANES, 1))
        out[i + d] = jnp.where(keep, pltpu.roll(a, LANES - d * PAIR_LANES, 1), b)
    return out


def _ssm_core_kernel(u_ref, toep_ref, bst_ref, cstt_ref, pow_ref, y_ref,
                     stage_in_ref, up_ref, s_ref, xprev_ref, yin_ref, carry_ref):
    bf16 = jnp.bfloat16
    n_quads = CHUNK // QUAD
    tile = lambda k: slice(k * LANES, (k + 1) * LANES)
    rows16 = 2 * SUBLANES

    @pl.when(pl.program_id(2) == 0)
    def _():
        carry_ref[...] = jnp.zeros_like(carry_ref)

    for c0 in range(0, SSM_BLOCK_CHUNKS, rows16):
        rows = slice(c0, c0 + rows16)
        for q in range(n_quads):
            out = _swap_stage([u_ref[0, 0, rows, tile(q * QUAD + i)] for i in range(QUAD)], 2)
            for i in range(QUAD):
                stage_in_ref[rows, tile(q * QUAD + i)] = out[i]
    for c0 in range(0, SSM_BLOCK_CHUNKS, rows16):
        rows = slice(c0, c0 + rows16)
        for q in range(n_quads):
            out = _swap_stage([stage_in_ref[rows, tile(q * QUAD + i)] for i in range(QUAD)], 1)
            for m in range(PAIRS_PER_SLAB):
                up_ref[m, rows, tile(q)] = out[m]

    for m in range(PAIRS_PER_SLAB):
        up = up_ref[m]
        s_ref[:, m * 2 * PAIR_STATES:(m + 1) * 2 * PAIR_STATES] = _dot(up, bst_ref[0, m])
        for nt in range(PAIR_IN // MXU_DIM):
            cols = slice(nt * MXU_DIM, (nt + 1) * MXU_DIM)
            k_hi = (nt + 1) * MXU_DIM
            yin_ref[m, :, cols] = _dot(up[:, 0:k_hi], toep_ref[0, m, 0:k_hi, cols])

    row = lax.broadcasted_iota(jnp.int32, (SUBLANES, PAIR_STATES), 0)
    for m in range(PAIRS_PER_SLAB):
        re = slice(m * 2 * PAIR_STATES, m * 2 * PAIR_STATES + PAIR_STATES)
        im = slice(m * 2 * PAIR_STATES + PAIR_STATES, (m + 1) * 2 * PAIR_STATES)
        bcast = lambda r, lanes: jnp.broadcast_to(pow_ref[r:r + 1, lanes], (SUBLANES, PAIR_STATES))
        steps = [(sh, bcast(r, re), bcast(r, im)) for sh, r in ((1, _ROW_A1), (2, _ROW_A2), (4, _ROW_A4))]
        pc_r = pow_ref[_ROW_CARRY:_ROW_CARRY + SUBLANES, re]
        pc_i = pow_ref[_ROW_CARRY:_ROW_CARRY + SUBLANES, im]
        cr = carry_ref[m, 0]
        ci = carry_ref[m, 1]
        for c0 in range(0, SSM_BLOCK_CHUNKS, rows16):
            prev = []
            for o in range(2):
                rows = slice(c0 + o * SUBLANES, c0 + (o + 1) * SUBLANES)
                er = s_ref[rows, re]
                ei = s_ref[rows, im]
                for sh, ar, ai in steps:
                    tr = jnp.where(row >= sh, pltpu.roll(er, sh, 0), 0.0)
                    ti = jnp.where(row >= sh, pltpu.roll(ei, sh, 0), 0.0)
                    er, ei = _cmul_add(ar, ai, tr, ti, er, ei)
                er, ei = _cmul_add(pc_r, pc_i, cr, ci, er, ei)
                prev.append((jnp.where(row == 0, cr, pltpu.roll(er, 1, 0)),
                             jnp.where(row == 0, ci, pltpu.roll(ei, 1, 0))))
                cr = jnp.broadcast_to(er[SUBLANES - 1:SUBLANES], (SUBLANES, PAIR_STATES))
                ci = jnp.broadcast_to(ei[SUBLANES - 1:SUBLANES], (SUBLANES, PAIR_STATES))
            rows = slice(c0, c0 + rows16)
            xprev_ref[rows, re] = jnp.concatenate([prev[0][0], prev[1][0]], axis=0).astype(bf16)
            xprev_ref[rows, im] = jnp.concatenate([prev[0][1], prev[1][1]], axis=0).astype(bf16)
        carry_ref[m, 0] = cr
        carry_ref[m, 1] = ci

    for m in range(PAIRS_PER_SLAB):
        y_ref[0, m] = yin_ref[m] + _dot_nt(xprev_ref[:, m * 2 * PAIR_STATES:(m + 1) * 2 * PAIR_STATES],
                                           cstt_ref[0, m])


def _ssm_core(u_rows, toep, bst, cstt, pow_tab):
    bsz, n_slabs, n_chunks, lanes = u_rows.shape
    seq = n_chunks * CHUNK
    assert seq % SSM_BLOCK_T == 0 and lanes == SLAB_IN
    bf16 = jnp.bfloat16
    f32 = jnp.float32
    tok_spec = pl.BlockSpec((1, 1, SSM_BLOCK_CHUNKS, SLAB_IN), lambda v, b, j: (b, v, j, 0))
    return pl.pallas_call(
        _ssm_core_kernel,
        grid=(N_SLABS, bsz, seq // SSM_BLOCK_T),
        in_specs=[tok_spec,
                  pl.BlockSpec((1,) + toep.shape[1:], lambda v, b, j: (v, 0, 0, 0)),
                  pl.BlockSpec((1,) + bst.shape[1:], lambda v, b, j: (v, 0, 0, 0)),
                  pl.BlockSpec((1,) + cstt.shape[1:], lambda v, b, j: (v, 0, 0, 0)),
                  pl.BlockSpec((_POW_ROWS, SLAB_STATES), lambda v, b, j: (0, v))],
        out_specs=pl.BlockSpec((1, PAIRS_PER_SLAB, SSM_BLOCK_CHUNKS, PAIR_IN), lambda v, b, j: (b, v, j, 0)),
        out_shape=jax.ShapeDtypeStruct((bsz, n_slabs * PAIRS_PER_SLAB, n_chunks, PAIR_IN), f32),
        scratch_shapes=[
            pltpu.VMEM((SSM_BLOCK_CHUNKS, SLAB_IN), bf16),
            pltpu.VMEM((PAIRS_PER_SLAB, SSM_BLOCK_CHUNKS, PAIR_IN), bf16),
            pltpu.VMEM((SSM_BLOCK_CHUNKS, SLAB_STATES), f32),
            pltpu.VMEM((SSM_BLOCK_CHUNKS, SLAB_STATES), bf16),
            pltpu.VMEM((PAIRS_PER_SLAB, SSM_BLOCK_CHUNKS, PAIR_IN), f32),
            pltpu.VMEM((PAIRS_PER_SLAB, 2, SUBLANES, PAIR_STATES), f32),
        ],
        compiler_params=pltpu.CompilerParams(
            dimension_semantics=("arbitrary", "arbitrary", "arbitrary"),
            vmem_limit_bytes=VMEM_LIMIT_BYTES),
        name="ssm_core",
    )(u_rows, toep, bst, cstt, pow_tab)


def _block_kernel(x_ref, mod_ref, npre_ref, npost_ref, win_ref, u_ref, yp_ref, dskip_ref, poolw_ref,
                  pscale_ref, gluw_ref, glub_ref, wbp_ref, wbs_ref, wout_ref, o_ref,
                  ext_ref, stage_ref, ytok_ref):
    j = pl.program_id(1)
    bf16 = jnp.bfloat16

    @pl.when(j == 0)
    def _():
        ext_ref[0:POOL_HALO, :] = jnp.zeros((POOL_HALO, D_MODEL), jnp.float32)

    gate = mod_ref[0, 2:3, :]
    for sub in range(BLOCK_T // SUB_T):
        r0 = sub * SUB_T
        x = x_ref[0, r0:r0 + SUB_T, :]
        hb = _rms_modulate(x, npre_ref[...], mod_ref[0, 1:2, :], mod_ref[0, 0:1, :]).astype(bf16)

        pn = _dot(hb, win_ref[:, 0:2 * D_MODEL])
        z_ssm = _dot(hb, win_ref[:, 3 * D_MODEL:4 * D_MODEL])
        gates = _dot(hb, win_ref[:, 4 * D_MODEL:6 * D_MODEL])
        u_pool = pn[:, 0:D_MODEL]

        e0 = POOL_HALO + r0
        ext_ref[e0:e0 + SUB_T, :] = u_pool
        pos = (j * BLOCK_T + r0 + 1
               + lax.broadcasted_iota(jnp.int32, (SUB_T, 1), 0)).astype(jnp.float32)
        mixed = []
        for g, w in enumerate(POOL_WINDOWS):
            cols = slice(g * POOL_GROUP_WIDTH, (g + 1) * POOL_GROUP_WIDTH)
            wsum = u_pool[:, cols]
            for lag in range(1, w):
                wsum = wsum + ext_ref[e0 - lag:e0 - lag + SUB_T, cols]
            pooled = wsum / jnp.minimum(pos, float(w)) - u_pool[:, cols]
            mixed.append(_dot(pooled.astype(bf16), poolw_ref[g]))
        y_pool = (jnp.concatenate(mixed, axis=-1) * pscale_ref[...]
                  * _silu(pn[:, D_MODEL:2 * D_MODEL]))

        chunk_rows = range(r0 // CHUNK, (r0 + SUB_T) // CHUNK, SUBLANES)
        tile = lambda k: slice(k * LANES, (k + 1) * LANES)
        for v in range(N_SLABS):
            pairs = [v * PAIRS_PER_SLAB + m for m in range(PAIRS_PER_SLAB)]
            for c0 in chunk_rows:
                rows = slice(c0, c0 + SUBLANES)
                for q in range(CHUNK // QUAD):
                    out = _swap_stage([yp_ref[0, p, rows, tile(q)] for p in pairs], 2)
                    for p, o in zip(pairs, out):
                        stage_ref[p, rows, tile(q)] = o
        for v in range(N_SLABS):
            pairs = [v * PAIRS_PER_SLAB + m for m in range(PAIRS_PER_SLAB)]
            for c0 in chunk_rows:
                rows = slice(c0, c0 + SUBLANES)
                for q in range(CHUNK // QUAD):
                    out = _swap_stage([stage_ref[p, rows, tile(q)] for p in pairs], 1)
                    for i in range(QUAD):
                        t = q * QUAD + i
                        ytok_ref[v, pl.ds(c0 * CHUNK + t, SUBLANES, stride=CHUNK), :] = (
                            out[i] + dskip_ref[v, :, tile(t)] * u_ref[0, v, rows, tile(t)])

        y = _gelu_tanh(jnp.concatenate([ytok_ref[v, r0:r0 + SUB_T, :] for v in range(N_SLABS)],
                                       axis=-1))
        y = y * _sigmoid(_dot(y.astype(bf16), gluw_ref[...]) + glub_ref[...])
        y_ssm = (y * _silu(z_ssm)).astype(bf16)

        merged = (_sigmoid(gates[:, 0:D_MODEL]) * _dot(y_pool.astype(bf16), wbp_ref[...])
                  + _sigmoid(gates[:, D_MODEL:2 * D_MODEL]) * _dot(y_ssm, wbs_ref[...]))
        out = _dot(merged.astype(bf16), wout_ref[...])
        rn = out * lax.rsqrt(jnp.mean(out * out, axis=-1, keepdims=True) + RMS_EPS) * npost_ref[...]
        o_ref[0, r0:r0 + SUB_T, :] = x + gate * rn
    ext_ref[0:POOL_HALO, :] = ext_ref[BLOCK_T:BLOCK_T + POOL_HALO, :]


def _layer(x, c, w_ada, b_ada, norm_pre, norm_post, w_in, pool_w, pool_scale, a_re, a_im, log_dt,
           b_re, b_im, c_re, c_im, d_skip, glu_w, glu_b, w_branch_pool, w_branch_ssm, w_out):
    bsz, seq, d = x.shape
    assert d == D_MODEL and seq % BLOCK_T == 0
    bf16 = jnp.bfloat16
    f32 = jnp.float32

    toep, bst, cstt, pow_tab = _ssm_prep(a_re, a_im, log_dt, b_re, b_im, c_re, c_im)
    c_pad = jnp.zeros((SUBLANES, d), f32).at[:bsz].set(c)
    mod = _adaln(c_pad, w_ada, b_ada)[:bsz].reshape(bsz, 3, d)
    w_in_b = w_in.astype(bf16)

    u_rows, u_rows_b = _ssm_in(x, mod, norm_pre, w_in_b[:, 2 * d:3 * d])
    y_pairs = _ssm_core(u_rows_b, toep, bst, cstt, pow_tab)
    d_rows = jnp.tile(d_skip.reshape(N_SLABS, 1, LANES), (1, 1, CHUNK))

    row = lambda v: v.reshape(1, d)
    tok_spec = pl.BlockSpec((1, BLOCK_T, d), lambda b, j: (b, j, 0))
    block_chunks = BLOCK_T // CHUNK
    u_spec = pl.BlockSpec((1, N_SLABS, block_chunks, SLAB_IN), lambda b, j: (b, 0, j, 0))
    y_spec = pl.BlockSpec((1, N_SLABS * PAIRS_PER_SLAB, block_chunks, PAIR_IN), lambda b, j: (b, 0, j, 0))
    operands = [
        (x, tok_spec),
        (mod, pl.BlockSpec((1, 3, d), lambda b, j: (b, 0, 0))),
        (row(norm_pre), None), (row(norm_post), None),
        (w_in_b, None),
        (u_rows, u_spec), (y_pairs, y_spec), (d_rows, None),
        (pool_w.astype(bf16), None), (row(pool_scale), None),
        (glu_w.astype(bf16), None), (row(glu_b), None),
        (w_branch_pool.astype(bf16), None), (w_branch_ssm.astype(bf16), None),
        (w_out.astype(bf16), None),
    ]
    arrays = [a for a, _ in operands]
    specs = [s if s is not None else _const_spec(a.shape) for a, s in operands]

    return pl.pallas_call(
        _block_kernel,
        grid=(bsz, seq // BLOCK_T),
        in_specs=specs,
        out_specs=tok_spec,
        out_shape=jax.ShapeDtypeStruct((bsz, seq, d), x.dtype),
        scratch_shapes=[
            pltpu.VMEM((BLOCK_T + POOL_HALO, d), f32),
            pltpu.VMEM((N_SLABS * PAIRS_PER_SLAB, BLOCK_T // CHUNK, PAIR_IN), f32),
            pltpu.VMEM((N_SLABS, BLOCK_T, LANES), f32),
        ],
        compiler_params=pltpu.CompilerParams(
            dimension_semantics=("arbitrary", "arbitrary"),
            vmem_limit_bytes=VMEM_LIMIT_BYTES),
        name="block",
    )(*arrays)


def kernel(x, c, w_ada, b_ada, norm_pre, norm_post, w_in, pool_w, pool_scale, ssm_a_re, ssm_a_im,
           ssm_log_dt, ssm_b_re, ssm_b_im, ssm_c_re, ssm_c_im, ssm_d, glu_w, glu_b, w_branch_pool,
           w_branch_ssm, w_out):
    for layer in range(w_in.shape[0]):
        x = _layer(x, c, w_ada[layer], b_ada[layer], norm_pre[layer], norm_post[layer],
                   w_in[layer], pool_w[layer], pool_scale[layer], ssm_a_re[layer],
                   ssm_a_im[layer], ssm_log_dt[layer], ssm_b_re[layer], ssm_b_im[layer],
                   ssm_c_re[layer], ssm_c_im[layer], ssm_d.reshape(ssm_d.shape[0], -1)[layer],
                   glu_w[layer], glu_b[layer], w_branch_pool[layer], w_branch_ssm[layer],
                   w_out[layer])
    return x
```

```python
import math

import numpy as np
import jax
import jax.numpy as jnp
from jax import lax
from jax.experimental import pallas as pl
from jax.experimental.pallas import tpu as pltpu

D_MODEL = 1024
POOL_WINDOWS = (2, 4, 8, 16)
POOL_GROUP_WIDTH = D_MODEL // len(POOL_WINDOWS)
POOL_HALO = 16
SSM_GROUP = 16
SSM_GROUPS = D_MODEL // SSM_GROUP
SSM_STATE = 64
RMS_EPS = 1e-6

SUBLANES = 8
LANES = 128
MXU_DIM = 256
CHUNK = 16
GROUPS_PER_SLAB = LANES // SSM_GROUP
N_SLABS = D_MODEL // LANES
PAIRS_PER_SLAB = GROUPS_PER_SLAB // 2
PAIR_STATES = 2 * SSM_STATE
SLAB_IN = CHUNK * LANES
PAIR_LANES = 2 * SSM_GROUP
PAIR_IN = CHUNK * PAIR_LANES
QUAD = LANES // PAIR_LANES
SLAB_STATES = PAIRS_PER_SLAB * 2 * PAIR_STATES
BLOCK_T = 512
SUB_T = 256
SSM_IN_BLOCK_T = 1024
SSM_BLOCK_T = 4096
SSM_BLOCK_CHUNKS = SSM_BLOCK_T // CHUNK
VMEM_LIMIT_BYTES = 56 * 1024 * 1024

_ROW_A1, _ROW_A2, _ROW_A4, _ROW_CARRY = 0, 1, 2, 8
_POW_ROWS = 16


def _sigmoid(v):
    return 0.5 * jnp.tanh(0.5 * v) + 0.5


def _silu(v):
    return v * _sigmoid(v)


def _gelu_tanh(v):
    c = math.sqrt(2.0 / math.pi)
    return 0.5 * v * (1.0 + jnp.tanh(c * (v + 0.044715 * (v * v * v))))


def _dot(a, b):
    return jnp.dot(a, b, preferred_element_type=jnp.float32)


def _dot_nt(a, b, precision=None):
    return lax.dot_general(a, b, (((1,), (1,)), ((), ())), precision=precision,
                           preferred_element_type=jnp.float32)


def _cmul_add(ar, ai, xr, xi, br, bi):
    return ar * xr - ai * xi + br, ar * xi + ai * xr + bi


def _rms_modulate(x, norm_gain, scale, shift):
    xn = x * lax.rsqrt(jnp.mean(x * x, axis=-1, keepdims=True) + RMS_EPS) * norm_gain
    return xn * (1.0 + scale) + shift


def _const_spec(shape):
    nd = len(shape)
    return pl.BlockSpec(shape, lambda *_: (0,) * nd, pipeline_mode=pl.Buffered(1))


def _ssm_prep_kernel(are_ref, aim_ref, ldt_ref, bre_ref, bim_ref, cre_ref, cim_ref,
                     toep_ref, bst_ref, cstt_ref, pow_ref):
    bf16 = jnp.bfloat16
    in_group = [lax.broadcasted_iota(jnp.int32, (SSM_GROUP, LANES), 1) // SSM_STATE == g
                for g in range(2)]
    lane_block = lax.broadcasted_iota(jnp.int32, (SSM_GROUP, LANES), 1) // SSM_GROUP

    def split(v_re, v_im, g):
        return jnp.concatenate([jnp.where(in_group[g], v_re, 0.0), jnp.where(in_group[g], v_im, 0.0)],
                               axis=1)

    pow_ref[...] = jnp.zeros_like(pow_ref)
    for m in range(PAIRS_PER_SLAB):
        lanes = slice(m * LANES, (m + 1) * LANES)
        dt = jnp.exp(ldt_ref[:, lanes])
        lam_re = jnp.minimum(are_ref[:, lanes], -1e-4)
        lam_im = aim_ref[:, lanes]
        l_re = lam_re * dt
        l_im = lam_im * dt

        def cmul(p, q):
            return p[0] * q[0] - p[1] * q[1], p[0] * q[1] + p[1] * q[0]

        mag = jnp.exp(l_re)
        pows = [(jnp.ones_like(l_re), jnp.zeros_like(l_re)), (mag * jnp.cos(l_im), mag * jnp.sin(l_im))]
        for _ in range(CHUNK - 1):
            pows.append(cmul(pows[-1], pows[1]))
        abar_re, abar_im = pows[1]
        den = lam_re * lam_re + lam_im * lam_im
        num_re = abar_re - 1.0
        f_re = (num_re * lam_re + abar_im * lam_im) / den
        f_im = (abar_im * lam_re - num_re * lam_im) / den
        b_re = bre_ref[:, lanes]
        b_im = bim_ref[:, lanes]
        bb_re = f_re * b_re - f_im * b_im
        bb_im = f_re * b_im + f_im * b_re
        c_re = cre_ref[:, lanes]
        c_im = cim_ref[:, lanes]

        for s in range(CHUNK):
            ar, ai = pows[CHUNK - 1 - s]
            w_re = ar * bb_re - ai * bb_im
            w_im = ar * bb_im + ai * bb_re
            for g in range(2):
                r0 = s * PAIR_LANES + g * SSM_GROUP
                bst_ref[0, m, r0:r0 + SSM_GROUP, :] = split(w_re, w_im, g).astype(bf16)

        z = []
        for n in range(CHUNK + 1):
            ar, ai = pows[n]
            z.append((ar * c_re - ai * c_im, -(ar * c_im + ai * c_re)))
        for t in range(CHUNK):
            for g in range(2):
                r0 = t * PAIR_LANES + g * SSM_GROUP
                cstt_ref[0, m, r0:r0 + SSM_GROUP, :] = split(z[t + 1][0], z[t + 1][1], g).astype(bf16)

        z_all = jnp.concatenate([jnp.concatenate([z[n][0], z[n][1]], axis=1) for n in range(CHUNK)],
                                axis=0)
        lane_pair = lax.broadcasted_iota(jnp.int32, (SSM_GROUP, PAIR_IN), 1)
        for g in range(2):
            lag = _dot_nt(split(bb_re, bb_im, g), z_all, precision=lax.Precision.HIGHEST)
            cols = []
            for q in range(CHUNK // QUAD):
                col = jnp.zeros((SSM_GROUP, LANES), jnp.float32)
                for i in range(QUAD):
                    n = q * QUAD + i
                    src = lag[:, (n // GROUPS_PER_SLAB) * LANES:(n // GROUPS_PER_SLAB + 1) * LANES]
                    dst_block = 2 * i + g
                    shift = (SSM_GROUP * (dst_block - n % GROUPS_PER_SLAB)) % LANES
                    moved = pltpu.roll(src, shift, 1) if shift else src
                    col = jnp.where(lane_block == dst_block, moved, col)
                cols.append(col)
            lag_pair = jnp.concatenate(cols, axis=1)
            for s in range(CHUNK):
                r0 = s * PAIR_LANES + g * SSM_GROUP
                if s == 0:
                    blk = lag_pair
                else:
                    blk = jnp.where(lane_pair >= s * PAIR_LANES,
                                    pltpu.roll(lag_pair, s * PAIR_LANES, 1), 0.0)
                toep_ref[0, m, r0:r0 + SSM_GROUP, :] = blk.astype(bf16)

        def pow_row(p):
            return jnp.concatenate([p[0], p[1]], axis=1)

        cols = slice(m * 2 * PAIR_STATES, (m + 1) * 2 * PAIR_STATES)
        a16 = pows[CHUNK]
        a32 = cmul(a16, a16)
        pow_ref[_ROW_A1:_ROW_A1 + 1, cols] = pow_row(a16)
        pow_ref[_ROW_A2:_ROW_A2 + 1, cols] = pow_row(a32)
        pow_ref[_ROW_A4:_ROW_A4 + 1, cols] = pow_row(cmul(a32, a32))
        carry_pow = a16
        for r in range(SUBLANES):
            pow_ref[_ROW_CARRY + r:_ROW_CARRY + r + 1, cols] = pow_row(carry_pow)
            carry_pow = cmul(carry_pow, a16)


def _ssm_prep(a_re, a_im, log_dt, b_re, b_im, c_re, c_im):
    g, p, h = SSM_GROUPS, SSM_STATE, SSM_GROUP
    slab_lanes = GROUPS_PER_SLAB * p
    flat = lambda v: v.reshape(1, g * p)
    row_spec = pl.BlockSpec((1, slab_lanes), lambda v: (0, v))
    mat_spec = pl.BlockSpec((h, slab_lanes), lambda v: (0, v))
    bf16 = jnp.bfloat16
    states = (N_SLABS, PAIRS_PER_SLAB, PAIR_IN, 2 * PAIR_STATES)
    toep = (N_SLABS, PAIRS_PER_SLAB, PAIR_IN, PAIR_IN)
    return pl.pallas_call(
        _ssm_prep_kernel,
        grid=(N_SLABS,),
        in_specs=[row_spec, row_spec, row_spec, mat_spec, mat_spec, mat_spec, mat_spec],
        out_specs=(pl.BlockSpec((1,) + toep[1:], lambda v: (v, 0, 0, 0)),
                   pl.BlockSpec((1,) + states[1:], lambda v: (v, 0, 0, 0)),
                   pl.BlockSpec((1,) + states[1:], lambda v: (v, 0, 0, 0)),
                   pl.BlockSpec((_POW_ROWS, SLAB_STATES), lambda v: (0, v))),
        out_shape=(jax.ShapeDtypeStruct(toep, bf16),
                   jax.ShapeDtypeStruct(states, bf16),
                   jax.ShapeDtypeStruct(states, bf16),
                   jax.ShapeDtypeStruct((_POW_ROWS, N_SLABS * SLAB_STATES), jnp.float32)),
        name="ssm_prep",
    )(flat(a_re), flat(a_im), flat(jnp.repeat(log_dt, p)),
      b_re.reshape(g * p, h).T, b_im.reshape(g * p, h).T,
      c_re.transpose(1, 0, 2).reshape(h, g * p), c_im.transpose(1, 0, 2).reshape(h, g * p))


def _adaln_kernel(c_ref, w_ref, b_ref, o_ref):
    o_ref[...] = _dot(_silu(c_ref[...]), w_ref[...]) + b_ref[...]


def _adaln(c_pad, w_ada, b_ada):
    rows = c_pad.shape[0]
    return pl.pallas_call(
        _adaln_kernel,
        grid=(3,),
        in_specs=[pl.BlockSpec((rows, D_MODEL), lambda j: (0, 0)),
                  pl.BlockSpec((D_MODEL, D_MODEL), lambda j: (0, j)),
                  pl.BlockSpec((1, D_MODEL), lambda j: (0, j))],
        out_specs=pl.BlockSpec((rows, D_MODEL), lambda j: (0, j)),
        out_shape=jax.ShapeDtypeStruct((rows, 3 * D_MODEL), jnp.float32),
        name="adaln",
    )(c_pad, w_ada, b_ada.reshape(1, 3 * D_MODEL))


def _ssm_in_kernel(x_ref, mod_ref, npre_ref, perm_ref, wu_ref, u_ref, ub_ref):
    sub_chunks = SUB_T // CHUNK
    for sub in range(SSM_IN_BLOCK_T // SUB_T):
        r0 = sub * SUB_T
        hb = _rms_modulate(x_ref[0, r0:r0 + SUB_T, :], npre_ref[...], mod_ref[0, 1:2, :],
                           mod_ref[0, 0:1, :]).astype(jnp.bfloat16)
        hp = _dot(perm_ref[...], hb).astype(jnp.bfloat16)
        u = _dot(hp, wu_ref[...])
        for v in range(N_SLABS):
            for t in range(CHUNK):
                piece = u[t * sub_chunks:(t + 1) * sub_chunks, v * LANES:(v + 1) * LANES]
                dst = (0, v, slice(sub * sub_chunks, (sub + 1) * sub_chunks), slice(t * LANES, (t + 1) * LANES))
                u_ref[dst] = piece
                ub_ref[dst] = piece.astype(jnp.bfloat16)


def _chunk_permutation():
    n_chunks = SUB_T // CHUNK
    p = np.zeros((SUB_T, SUB_T), np.float32)
    for c in range(n_chunks):
        for t in range(CHUNK):
            p[t * n_chunks + c, c * CHUNK + t] = 1.0
    return p


def _ssm_in(x, mod, norm_pre, w_u):
    bsz, seq, d = x.shape
    rows_spec = pl.BlockSpec((1, N_SLABS, SSM_IN_BLOCK_T // CHUNK, SLAB_IN), lambda b, j: (b, 0, j, 0))
    rows_shape = (bsz, N_SLABS, seq // CHUNK, SLAB_IN)
    return pl.pallas_call(
        _ssm_in_kernel,
        grid=(bsz, seq // SSM_IN_BLOCK_T),
        in_specs=[pl.BlockSpec((1, SSM_IN_BLOCK_T, d), lambda b, j: (b, j, 0)),
                  pl.BlockSpec((1, 3, d), lambda b, j: (b, 0, 0)),
                  _const_spec((1, d)), _const_spec((SUB_T, SUB_T)), _const_spec((d, d))],
        out_specs=(rows_spec, rows_spec),
        out_shape=(jax.ShapeDtypeStruct(rows_shape, jnp.float32),
                   jax.ShapeDtypeStruct(rows_shape, jnp.bfloat16)),
        compiler_params=pltpu.CompilerParams(dimension_semantics=("arbitrary", "arbitrary"),
                                             vmem_limit_bytes=VMEM_LIMIT_BYTES),
        name="ssm_in",
    )(x, mod, norm_pre.reshape(1, d), jnp.asarray(_chunk_permutation(), jnp.bfloat16), w_u)


def _swap_stage(tiles, d):
    block = lax.broadcasted_iota(jnp.int32, tiles[0].shape, 1) // PAIR_LANES
    keep = (block & d) == 0
    out = list(tiles)
    for i in range(PAIRS_PER_SLAB):
        if i & d:
            continue
        a, b = tiles[i], tiles[i + d]
        out[i] = jnp.where(keep, a, pltpu.roll(b, d * PAIR_LANES, 1))
        out[i + d] = jnp.where(keep, pltpu.roll(a, LANES - d * PAIR_LANES, 1), b)
    return out


def _ssm_core_kernel(u_ref, toep_ref, bst_ref, cstt_ref, pow_ref, y_ref,
                     stage_in_ref, up_ref, s_ref, xprev_ref, yin_ref, carry_ref):
    bf16 = jnp.bfloat16
    n_quads = CHUNK // QUAD
    tile = lambda k: slice(k * LANES, (k + 1) * LANES)
    rows16 = 2 * SUBLANES

    @pl.when(pl.program_id(2) == 0)
    def _():
        carry_ref[...] = jnp.zeros_like(carry_ref)

    for c0 in range(0, SSM_BLOCK_CHUNKS, rows16):
        rows = slice(c0, c0 + rows16)
        for q in range(n_quads):
            out = _swap_stage([u_ref[0, 0, rows, tile(q * QUAD + i)] for i in range(QUAD)], 2)
            for i in range(QUAD):
                stage_in_ref[rows, tile(q * QUAD + i)] = out[i]
    for c0 in range(0, SSM_BLOCK_CHUNKS, rows16):
        rows = slice(c0, c0 + rows16)
        for q in range(n_quads):
            out = _swap_stage([stage_in_ref[rows, tile(q * QUAD + i)] for i in range(QUAD)], 1)
            for m in range(PAIRS_PER_SLAB):
                up_ref[m, rows, tile(q)] = out[m]

    for m in range(PAIRS_PER_SLAB):
        up = up_ref[m]
        s_ref[:, m * 2 * PAIR_STATES:(m + 1) * 2 * PAIR_STATES] = _dot(up, bst_ref[0, m])
        for nt in range(PAIR_IN // MXU_DIM):
            cols = slice(nt * MXU_DIM, (nt + 1) * MXU_DIM)
            k_hi = (nt + 1) * MXU_DIM
            yin_ref[m, :, cols] = _dot(up[:, 0:k_hi], toep_ref[0, m, 0:k_hi, cols])

    row = lax.broadcasted_iota(jnp.int32, (SUBLANES, PAIR_STATES), 0)
    for m in range(PAIRS_PER_SLAB):
        re = slice(m * 2 * PAIR_STATES, m * 2 * PAIR_STATES + PAIR_STATES)
        im = slice(m * 2 * PAIR_STATES + PAIR_STATES, (m + 1) * 2 * PAIR_STATES)
        bcast = lambda r, lanes: jnp.broadcast_to(pow_ref[r:r + 1, lanes], (SUBLANES, PAIR_STATES))
        steps = [(sh, bcast(r, re), bcast(r, im)) for sh, r in ((1, _ROW_A1), (2, _ROW_A2), (4, _ROW_A4))]
        pc_r = pow_ref[_ROW_CARRY:_ROW_CARRY + SUBLANES, re]
        pc_i = pow_ref[_ROW_CARRY:_ROW_CARRY + SUBLANES, im]
        cr = carry_ref[m, 0]
        ci = carry_ref[m, 1]
        for c0 in range(0, SSM_BLOCK_CHUNKS, rows16):
            prev = []
            for o in range(2):
                rows = slice(c0 + o * SUBLANES, c0 + (o + 1) * SUBLANES)
                er = s_ref[rows, re]
                ei = s_ref[rows, im]
                for sh, ar, ai in steps:
                    tr = jnp.where(row >= sh, pltpu.roll(er, sh, 0), 0.0)
                    ti = jnp.where(row >= sh, pltpu.roll(ei, sh, 0), 0.0)
                    er, ei = _cmul_add(ar, ai, tr, ti, er, ei)
                er, ei = _cmul_add(pc_r, pc_i, cr, ci, er, ei)
                prev.append((jnp.where(row == 0, cr, pltpu.roll(er, 1, 0)),
                             jnp.where(row == 0, ci, pltpu.roll(ei, 1, 0))))
                cr = jnp.broadcast_to(er[SUBLANES - 1:SUBLANES], (SUBLANES, PAIR_STATES))
                ci = jnp.broadcast_to(ei[SUBLANES - 1:SUBLANES], (SUBLANES, PAIR_STATES))
            rows = slice(c0, c0 + rows16)
            xprev_ref[rows, re] = jnp.concatenate([prev[0][0], prev[1][0]], axis=0).astype(bf16)
            xprev_ref[rows, im] = jnp.concatenate([prev[0][1], prev[1][1]], axis=0).astype(bf16)
        carry_ref[m, 0] = cr
        carry_ref[m, 1] = ci

    for m in range(PAIRS_PER_SLAB):
        y_ref[0, m] = yin_ref[m] + _dot_nt(xprev_ref[:, m * 2 * PAIR_STATES:(m + 1) * 2 * PAIR_STATES],
                                           cstt_ref[0, m])


def _ssm_core(u_rows, toep, bst, cstt, pow_tab):
    bsz, n_slabs, n_chunks, lanes = u_rows.shape
    seq = n_chunks * CHUNK
    assert seq % SSM_BLOCK_T == 0 and lanes == SLAB_IN
    bf16 = jnp.bfloat16
    f32 = jnp.float32
    tok_spec = pl.BlockSpec((1, 1, SSM_BLOCK_CHUNKS, SLAB_IN), lambda v, b, j: (b, v, j, 0))
    return pl.pallas_call(
        _ssm_core_kernel,
        grid=(N_SLABS, bsz, seq // SSM_BLOCK_T),
        in_specs=[tok_spec,
                  pl.BlockSpec((1,) + toep.shape[1:], lambda v, b, j: (v, 0, 0, 0)),
                  pl.BlockSpec((1,) + bst.shape[1:], lambda v, b, j: (v, 0, 0, 0)),
                  pl.BlockSpec((1,) + cstt.shape[1:], lambda v, b, j: (v, 0, 0, 0)),
                  pl.BlockSpec((_POW_ROWS, SLAB_STATES), lambda v, b, j: (0, v))],
        out_specs=pl.BlockSpec((1, PAIRS_PER_SLAB, SSM_BLOCK_CHUNKS, PAIR_IN), lambda v, b, j: (b, v, j, 0)),
        out_shape=jax.ShapeDtypeStruct((bsz, n_slabs * PAIRS_PER_SLAB, n_chunks, PAIR_IN), f32),
        scratch_shapes=[
            pltpu.VMEM((SSM_BLOCK_CHUNKS, SLAB_IN), bf16),
            pltpu.VMEM((PAIRS_PER_SLAB, SSM_BLOCK_CHUNKS, PAIR_IN), bf16),
            pltpu.VMEM((SSM_BLOCK_CHUNKS, SLAB_STATES), f32),
            pltpu.VMEM((SSM_BLOCK_CHUNKS, SLAB_STATES), bf16),
            pltpu.VMEM((PAIRS_PER_SLAB, SSM_BLOCK_CHUNKS, PAIR_IN), f32),
            pltpu.VMEM((PAIRS_PER_SLAB, 2, SUBLANES, PAIR_STATES), f32),
        ],
        compiler_params=pltpu.CompilerParams(
            dimension_semantics=("arbitrary", "arbitrary", "arbitrary"),
            vmem_limit_bytes=VMEM_LIMIT_BYTES),
        name="ssm_core",
    )(u_rows, toep, bst, cstt, pow_tab)


def _block_kernel(x_ref, mod_ref, npre_ref, npost_ref, win_ref, u_ref, yp_ref, dskip_ref, poolw_ref,
                  pscale_ref, gluw_ref, glub_ref, wbp_ref, wbs_ref, wout_ref, o_ref,
                  hist_ref, stage_ref, ytok_ref):
    j = pl.program_id(1)
    bf16 = jnp.bfloat16

    @pl.when(j == 0)
    def _():
        hist_ref[...] = jnp.zeros_like(hist_ref)

    gate = mod_ref[0, 2:3, :]
    for sub in range(BLOCK_T // SUB_T):
        r0 = sub * SUB_T
        x = x_ref[0, r0:r0 + SUB_T, :]
        hb = _rms_modulate(x, npre_ref[...], mod_ref[0, 1:2, :], mod_ref[0, 0:1, :]).astype(bf16)

        pn = _dot(hb, win_ref[:, 0:2 * D_MODEL])
        z_ssm = _dot(hb, win_ref[:, 3 * D_MODEL:4 * D_MODEL])
        gates = _dot(hb, win_ref[:, 4 * D_MODEL:6 * D_MODEL])
        u_pool = pn[:, 0:D_MODEL]

        pos = (j * BLOCK_T + r0 + 1
               + lax.broadcasted_iota(jnp.int32, (SUB_T, 1), 0)).astype(jnp.float32)
        mixed = []
        for g, w in enumerate(POOL_WINDOWS):
            cols = slice(g * POOL_GROUP_WIDTH, (g + 1) * POOL_GROUP_WIDTH)
            acc = jnp.concatenate([hist_ref[:, cols], u_pool[:, cols]], axis=0)
            span = 1
            while span < w:
                acc = acc + pltpu.roll(acc, span, 0)
                span *= 2
            inv_count = 1.0 / jnp.minimum(pos, float(w))
            pooled = acc[POOL_HALO:] * inv_count - u_pool[:, cols]
            mixed.append(_dot(pooled.astype(bf16), poolw_ref[g]))
        hist_ref[...] = u_pool[SUB_T - POOL_HALO:, :]
        y_pool = (jnp.concatenate(mixed, axis=-1) * pscale_ref[...]
                  * _silu(pn[:, D_MODEL:2 * D_MODEL]))

        chunk_rows = range(r0 // CHUNK, (r0 + SUB_T) // CHUNK, SUBLANES)
        tile = lambda k: slice(k * LANES, (k + 1) * LANES)
        for v in range(N_SLABS):
            pairs = [v * PAIRS_PER_SLAB + m for m in range(PAIRS_PER_SLAB)]
            for c0 in chunk_rows:
                rows = slice(c0, c0 + SUBLANES)
                for q in range(CHUNK // QUAD):
                    out = _swap_stage([yp_ref[0, p, rows, tile(q)] for p in pairs], 2)
                    for p, o in zip(pairs, out):
                        stage_ref[p, rows, tile(q)] = o
        for v in range(N_SLABS):
            pairs = [v * PAIRS_PER_SLAB + m for m in range(PAIRS_PER_SLAB)]
            for c0 in chunk_rows:
                rows = slice(c0, c0 + SUBLANES)
                for q in range(CHUNK // QUAD):
                    out = _swap_stage([stage_ref[p, rows, tile(q)] for p in pairs], 1)
                    for i in range(QUAD):
                        t = q * QUAD + i
                        ytok_ref[v, pl.ds(c0 * CHUNK + t, SUBLANES, stride=CHUNK), :] = (
                            out[i] + dskip_ref[v, :, tile(t)] * u_ref[0, v, rows, tile(t)])

        y = _gelu_tanh(jnp.concatenate([ytok_ref[v, r0:r0 + SUB_T, :] for v in range(N_SLABS)],
                                       axis=-1))
        y = y * _sigmoid(_dot(y.astype(bf16), gluw_ref[...]) + glub_ref[...])
        y_ssm = (y * _silu(z_ssm)).astype(bf16)

        merged = (_sigmoid(gates[:, 0:D_MODEL]) * _dot(y_pool.astype(bf16), wbp_ref[...])
                  + _sigmoid(gates[:, D_MODEL:2 * D_MODEL]) * _dot(y_ssm, wbs_ref[...]))
        out = _dot(merged.astype(bf16), wout_ref[...])
        rn = out * lax.rsqrt(jnp.mean(out * out, axis=-1, keepdims=True) + RMS_EPS) * npost_ref[...]
        o_ref[0, r0:r0 + SUB_T, :] = x + gate * rn


def _layer(x, c, w_ada, b_ada, norm_pre, norm_post, w_in, pool_w, pool_scale, a_re, a_im, log_dt,
           b_re, b_im, c_re, c_im, d_skip, glu_w, glu_b, w_branch_pool, w_branch_ssm, w_out):
    bsz, seq, d = x.shape
    assert d == D_MODEL and seq % BLOCK_T == 0
    bf16 = jnp.bfloat16
    f32 = jnp.float32

    toep, bst, cstt, pow_tab = _ssm_prep(a_re, a_im, log_dt, b_re, b_im, c_re, c_im)
    c_pad = jnp.zeros((SUBLANES, d), f32).at[:bsz].set(c)
    mod = _adaln(c_pad, w_ada, b_ada)[:bsz].reshape(bsz, 3, d)
    w_in_b = w_in.astype(bf16)

    u_rows, u_rows_b = _ssm_in(x, mod, norm_pre, w_in_b[:, 2 * d:3 * d])
    y_pairs = _ssm_core(u_rows_b, toep, bst, cstt, pow_tab)
    d_rows = jnp.tile(d_skip.reshape(N_SLABS, 1, LANES), (1, 1, CHUNK))

    row = lambda v: v.reshape(1, d)
    tok_spec = pl.BlockSpec((1, BLOCK_T, d), lambda b, j: (b, j, 0))
    block_chunks = BLOCK_T // CHUNK
    u_spec = pl.BlockSpec((1, N_SLABS, block_chunks, SLAB_IN), lambda b, j: (b, 0, j, 0))
    y_spec = pl.BlockSpec((1, N_SLABS * PAIRS_PER_SLAB, block_chunks, PAIR_IN), lambda b, j: (b, 0, j, 0))
    operands = [
        (x, tok_spec),
        (mod, pl.BlockSpec((1, 3, d), lambda b, j: (b, 0, 0))),
        (row(norm_pre), None), (row(norm_post), None),
        (w_in_b, None),
        (u_rows, u_spec), (y_pairs, y_spec), (d_rows, None),
        (pool_w.astype(bf16), None), (row(pool_scale), None),
        (glu_w.astype(bf16), None), (row(glu_b), None),
        (w_branch_pool.astype(bf16), None), (w_branch_ssm.astype(bf16), None),
        (w_out.astype(bf16), None),
    ]
    arrays = [a for a, _ in operands]
    specs = [s if s is not None else _const_spec(a.shape) for a, s in operands]

    return pl.pallas_call(
        _block_kernel,
        grid=(bsz, seq // BLOCK_T),
        in_specs=specs,
        out_specs=tok_spec,
        out_shape=jax.ShapeDtypeStruct((bsz, seq, d), x.dtype),
        scratch_shapes=[
            pltpu.VMEM((POOL_HALO, d), f32),
            pltpu.VMEM((N_SLABS * PAIRS_PER_SLAB, BLOCK_T // CHUNK, PAIR_IN), f32),
            pltpu.VMEM((N_SLABS, BLOCK_T, LANES), f32),
        ],
        compiler_params=pltpu.CompilerParams(
            dimension_semantics=("arbitrary", "arbitrary"),
            vmem_limit_bytes=VMEM_LIMIT_BYTES),
        name="block",
    )(*arrays)


def kernel(x, c, w_ada, b_ada, norm_pre, norm_post, w_in, pool_w, pool_scale, ssm_a_re, ssm_a_im,
           ssm_log_dt, ssm_b_re, ssm_b_im, ssm_c_re, ssm_c_im, ssm_d, glu_w, glu_b, w_branch_pool,
           w_branch_ssm, w_out):
    for layer in range(w_in.shape[0]):
        x = _layer(x, c, w_ada[layer], b_ada[layer], norm_pre[layer], norm_post[layer],
                   w_in[layer], pool_w[layer], pool_scale[layer], ssm_a_re[layer],
                   ssm_a_im[layer], ssm_log_dt[layer], ssm_b_re[layer], ssm_b_im[layer],
                   ssm_c_re[layer], ssm_c_im[layer], ssm_d.reshape(ssm_d.shape[0], -1)[layer],
                   glu_w[layer], glu_b[layer], w_branch_pool[layer], w_branch_ssm[layer],
                   w_out[layer])
    return x
```

```python
import math

import numpy as np
import jax
import jax.numpy as jnp
from jax import lax
from jax.experimental import pallas as pl
from jax.experimental.pallas import tpu as pltpu

D_MODEL = 1024
POOL_WINDOWS = (2, 4, 8, 16)
POOL_GROUP_WIDTH = D_MODEL // len(POOL_WINDOWS)
POOL_HALO = 16
SSM_GROUP = 16
SSM_GROUPS = D_MODEL // SSM_GROUP
SSM_STATE = 64
RMS_EPS = 1e-6

SUBLANES = 8
LANES = 128
MXU_DIM = 256
CHUNK = 16
GROUPS_PER_SLAB = LANES // SSM_GROUP
N_SLABS = D_MODEL // LANES
PAIRS_PER_SLAB = GROUPS_PER_SLAB // 2
PAIR_STATES = 2 * SSM_STATE
SLAB_IN = CHUNK * LANES
PAIR_LANES = 2 * SSM_GROUP
PAIR_IN = CHUNK * PAIR_LANES
QUAD = LANES // PAIR_LANES
SLAB_STATES = PAIRS_PER_SLAB * 2 * PAIR_STATES
BLOCK_T = 512
SUB_T = 256
SSM_IN_BLOCK_T = 1024
SSM_BLOCK_T = 4096
SSM_BLOCK_CHUNKS = SSM_BLOCK_T // CHUNK
VMEM_LIMIT_BYTES = 56 * 1024 * 1024

_ROW_A1, _ROW_A2, _ROW_A4, _ROW_CARRY = 0, 1, 2, 8
_POW_ROWS = 16


def _sigmoid(v):
    return 0.5 * jnp.tanh(0.5 * v) + 0.5


def _silu(v):
    return v * _sigmoid(v)


def _sigmoid_of_half(h):
    return 0.5 * jnp.tanh(h) + 0.5


def _silu_of_half(h):
    return h * jnp.tanh(h) + h


def _gelu_tanh(v):
    c = math.sqrt(2.0 / math.pi)
    return 0.5 * v * (1.0 + jnp.tanh(c * (v + 0.044715 * (v * v * v))))


def _dot(a, b):
    return jnp.dot(a, b, preferred_element_type=jnp.float32)


def _dot_nt(a, b, precision=None):
    return lax.dot_general(a, b, (((1,), (1,)), ((), ())), precision=precision,
                           preferred_element_type=jnp.float32)


def _cmul_add(ar, ai, xr, xi, br, bi):
    return ar * xr - ai * xi + br, ar * xi + ai * xr + bi


def _rms_modulate(x, norm_gain, scale, shift):
    xn = x * lax.rsqrt(jnp.mean(x * x, axis=-1, keepdims=True) + RMS_EPS) * norm_gain
    return xn * (1.0 + scale) + shift


def _const_spec(shape):
    nd = len(shape)
    return pl.BlockSpec(shape, lambda *_: (0,) * nd, pipeline_mode=pl.Buffered(1))


def _ssm_prep_kernel(are_ref, aim_ref, ldt_ref, bre_ref, bim_ref, cre_ref, cim_ref,
                     toep_ref, bst_ref, cstt_ref, pow_ref):
    bf16 = jnp.bfloat16
    in_group = [lax.broadcasted_iota(jnp.int32, (SSM_GROUP, LANES), 1) // SSM_STATE == g
                for g in range(2)]
    lane_block = lax.broadcasted_iota(jnp.int32, (SSM_GROUP, LANES), 1) // SSM_GROUP

    def split(v_re, v_im, g):
        return jnp.concatenate([jnp.where(in_group[g], v_re, 0.0), jnp.where(in_group[g], v_im, 0.0)],
                               axis=1)

    pow_ref[...] = jnp.zeros_like(pow_ref)
    for m in range(PAIRS_PER_SLAB):
        lanes = slice(m * LANES, (m + 1) * LANES)
        dt = jnp.exp(ldt_ref[:, lanes])
        lam_re = jnp.minimum(are_ref[:, lanes], -1e-4)
        lam_im = aim_ref[:, lanes]
        l_re = lam_re * dt
        l_im = lam_im * dt

        def cmul(p, q):
            return p[0] * q[0] - p[1] * q[1], p[0] * q[1] + p[1] * q[0]

        mag = jnp.exp(l_re)
        pows = [(jnp.ones_like(l_re), jnp.zeros_like(l_re)), (mag * jnp.cos(l_im), mag * jnp.sin(l_im))]
        for _ in range(CHUNK - 1):
            pows.append(cmul(pows[-1], pows[1]))
        abar_re, abar_im = pows[1]
        den = lam_re * lam_re + lam_im * lam_im
        num_re = abar_re - 1.0
        f_re = (num_re * lam_re + abar_im * lam_im) / den
        f_im = (abar_im * lam_re - num_re * lam_im) / den
        b_re = bre_ref[:, lanes]
        b_im = bim_ref[:, lanes]
        bb_re = f_re * b_re - f_im * b_im
        bb_im = f_re * b_im + f_im * b_re
        c_re = cre_ref[:, lanes]
        c_im = cim_ref[:, lanes]

        for s in range(CHUNK):
            ar, ai = pows[CHUNK - 1 - s]
            w_re = ar * bb_re - ai * bb_im
            w_im = ar * bb_im + ai * bb_re
            for g in range(2):
                r0 = s * PAIR_LANES + g * SSM_GROUP
                bst_ref[0, m, r0:r0 + SSM_GROUP, :] = split(w_re, w_im, g).astype(bf16)

        z = []
        for n in range(CHUNK + 1):
            ar, ai = pows[n]
            z.append((ar * c_re - ai * c_im, -(ar * c_im + ai * c_re)))
        for t in range(CHUNK):
            for g in range(2):
                r0 = t * PAIR_LANES + g * SSM_GROUP
                cstt_ref[0, m, r0:r0 + SSM_GROUP, :] = split(z[t + 1][0], z[t + 1][1], g).astype(bf16)

        z_all = jnp.concatenate([jnp.concatenate([z[n][0], z[n][1]], axis=1) for n in range(CHUNK)],
                                axis=0)
        lane_pair = lax.broadcasted_iota(jnp.int32, (SSM_GROUP, PAIR_IN), 1)
        for g in range(2):
            lag = _dot_nt(split(bb_re, bb_im, g), z_all, precision=lax.Precision.HIGHEST)
            cols = []
            for q in range(CHUNK // QUAD):
                col = jnp.zeros((SSM_GROUP, LANES), jnp.float32)
                for i in range(QUAD):
                    n = q * QUAD + i
                    src = lag[:, (n // GROUPS_PER_SLAB) * LANES:(n // GROUPS_PER_SLAB + 1) * LANES]
                    dst_block = 2 * i + g
                    shift = (SSM_GROUP * (dst_block - n % GROUPS_PER_SLAB)) % LANES
                    moved = pltpu.roll(src, shift, 1) if shift else src
                    col = jnp.where(lane_block == dst_block, moved, col)
                cols.append(col)
            lag_pair = jnp.concatenate(cols, axis=1)
            for s in range(CHUNK):
                r0 = s * PAIR_LANES + g * SSM_GROUP
                if s == 0:
                    blk = lag_pair
                else:
                    blk = jnp.where(lane_pair >= s * PAIR_LANES,
                                    pltpu.roll(lag_pair, s * PAIR_LANES, 1), 0.0)
                toep_ref[0, m, r0:r0 + SSM_GROUP, :] = blk.astype(bf16)

        def pow_row(p):
            return jnp.concatenate([p[0], p[1]], axis=1)

        cols = slice(m * 2 * PAIR_STATES, (m + 1) * 2 * PAIR_STATES)
        a16 = pows[CHUNK]
        a32 = cmul(a16, a16)
        pow_ref[_ROW_A1:_ROW_A1 + 1, cols] = pow_row(a16)
        pow_ref[_ROW_A2:_ROW_A2 + 1, cols] = pow_row(a32)
        pow_ref[_ROW_A4:_ROW_A4 + 1, cols] = pow_row(cmul(a32, a32))
        carry_pow = a16
        for r in range(SUBLANES):
            pow_ref[_ROW_CARRY + r:_ROW_CARRY + r + 1, cols] = pow_row(carry_pow)
            carry_pow = cmul(carry_pow, a16)


def _ssm_prep(a_re, a_im, log_dt, b_re, b_im, c_re, c_im):
    g, p, h = SSM_GROUPS, SSM_STATE, SSM_GROUP
    slab_lanes = GROUPS_PER_SLAB * p
    flat = lambda v: v.reshape(1, g * p)
    row_spec = pl.BlockSpec((1, slab_lanes), lambda v: (0, v))
    mat_spec = pl.BlockSpec((h, slab_lanes), lambda v: (0, v))
    bf16 = jnp.bfloat16
    states = (N_SLABS, PAIRS_PER_SLAB, PAIR_IN, 2 * PAIR_STATES)
    toep = (N_SLABS, PAIRS_PER_SLAB, PAIR_IN, PAIR_IN)
    return pl.pallas_call(
        _ssm_prep_kernel,
        grid=(N_SLABS,),
        in_specs=[row_spec, row_spec, row_spec, mat_spec, mat_spec, mat_spec, mat_spec],
        out_specs=(pl.BlockSpec((1,) + toep[1:], lambda v: (v, 0, 0, 0)),
                   pl.BlockSpec((1,) + states[1:], lambda v: (v, 0, 0, 0)),
                   pl.BlockSpec((1,) + states[1:], lambda v: (v, 0, 0, 0)),
                   pl.BlockSpec((_POW_ROWS, SLAB_STATES), lambda v: (0, v))),
        out_shape=(jax.ShapeDtypeStruct(toep, bf16),
                   jax.ShapeDtypeStruct(states, bf16),
                   jax.ShapeDtypeStruct(states, bf16),
                   jax.ShapeDtypeStruct((_POW_ROWS, N_SLABS * SLAB_STATES), jnp.float32)),
        name="ssm_prep",
    )(flat(a_re), flat(a_im), flat(jnp.repeat(log_dt, p)),
      b_re.reshape(g * p, h).T, b_im.reshape(g * p, h).T,
      c_re.transpose(1, 0, 2).reshape(h, g * p), c_im.transpose(1, 0, 2).reshape(h, g * p))


def _adaln_kernel(c_ref, w_ref, b_ref, o_ref):
    o_ref[...] = _dot(_silu(c_ref[...]), w_ref[...]) + b_ref[...]


def _adaln(c_pad, w_ada, b_ada):
    rows = c_pad.shape[0]
    return pl.pallas_call(
        _adaln_kernel,
        grid=(3,),
        in_specs=[pl.BlockSpec((rows, D_MODEL), lambda j: (0, 0)),
                  pl.BlockSpec((D_MODEL, D_MODEL), lambda j: (0, j)),
                  pl.BlockSpec((1, D_MODEL), lambda j: (0, j))],
        out_specs=pl.BlockSpec((rows, D_MODEL), lambda j: (0, j)),
        out_shape=jax.ShapeDtypeStruct((rows, 3 * D_MODEL), jnp.float32),
        name="adaln",
    )(c_pad, w_ada, b_ada.reshape(1, 3 * D_MODEL))


def _ssm_in_kernel(x_ref, mod_ref, npre_ref, perm_ref, wu_ref, u_ref, ub_ref):
    sub_chunks = SUB_T // CHUNK
    for sub in range(SSM_IN_BLOCK_T // SUB_T):
        r0 = sub * SUB_T
        hb = _rms_modulate(x_ref[0, r0:r0 + SUB_T, :], npre_ref[...], mod_ref[0, 1:2, :],
                           mod_ref[0, 0:1, :]).astype(jnp.bfloat16)
        hp = _dot(perm_ref[...], hb).astype(jnp.bfloat16)
        u = _dot(hp, wu_ref[...])
        for v in range(N_SLABS):
            for t in range(CHUNK):
                piece = u[t * sub_chunks:(t + 1) * sub_chunks, v * LANES:(v + 1) * LANES]
                dst = (0, v, slice(sub * sub_chunks, (sub + 1) * sub_chunks), slice(t * LANES, (t + 1) * LANES))
                u_ref[dst] = piece
                ub_ref[dst] = piece.astype(jnp.bfloat16)


def _chunk_permutation():
    n_chunks = SUB_T // CHUNK
    p = np.zeros((SUB_T, SUB_T), np.float32)
    for c in range(n_chunks):
        for t in range(CHUNK):
            p[t * n_chunks + c, c * CHUNK + t] = 1.0
    return p


def _ssm_in(x, mod, norm_pre, w_u):
    bsz, seq, d = x.shape
    rows_spec = pl.BlockSpec((1, N_SLABS, SSM_IN_BLOCK_T // CHUNK, SLAB_IN), lambda b, j: (b, 0, j, 0))
    rows_shape = (bsz, N_SLABS, seq // CHUNK, SLAB_IN)
    return pl.pallas_call(
        _ssm_in_kernel,
        grid=(bsz, seq // SSM_IN_BLOCK_T),
        in_specs=[pl.BlockSpec((1, SSM_IN_BLOCK_T, d), lambda b, j: (b, j, 0)),
                  pl.BlockSpec((1, 3, d), lambda b, j: (b, 0, 0)),
                  _const_spec((1, d)), _const_spec((SUB_T, SUB_T)), _const_spec((d, d))],
        out_specs=(rows_spec, rows_spec),
        out_shape=(jax.ShapeDtypeStruct(rows_shape, jnp.float32),
                   jax.ShapeDtypeStruct(rows_shape, jnp.bfloat16)),
        compiler_params=pltpu.CompilerParams(dimension_semantics=("arbitrary", "arbitrary"),
                                             vmem_limit_bytes=VMEM_LIMIT_BYTES),
        name="ssm_in",
    )(x, mod, norm_pre.reshape(1, d), jnp.asarray(_chunk_permutation(), jnp.bfloat16), w_u)


def _swap_stage(tiles, d):
    block = lax.broadcasted_iota(jnp.int32, tiles[0].shape, 1) // PAIR_LANES
    keep = (block & d) == 0
    out = list(tiles)
    for i in range(PAIRS_PER_SLAB):
        if i & d:
            continue
        a, b = tiles[i], tiles[i + d]
        out[i] = jnp.where(keep, a, pltpu.roll(b, d * PAIR_LANES, 1))
        out[i + d] = jnp.where(keep, pltpu.roll(a, LANES - d * PAIR_LANES, 1), b)
    return out


def _ssm_core_kernel(u_ref, toep_ref, bst_ref, cstt_ref, pow_ref, y_ref,
                     stage_in_ref, up_ref, s_ref, xprev_ref, yin_ref, carry_ref):
    bf16 = jnp.bfloat16
    n_quads = CHUNK // QUAD
    tile = lambda k: slice(k * LANES, (k + 1) * LANES)
    rows16 = 2 * SUBLANES

    @pl.when(pl.program_id(2) == 0)
    def _():
        carry_ref[...] = jnp.zeros_like(carry_ref)

    for c0 in range(0, SSM_BLOCK_CHUNKS, rows16):
        rows = slice(c0, c0 + rows16)
        for q in range(n_quads):
            out = _swap_stage([u_ref[0, 0, rows, tile(q * QUAD + i)] for i in range(QUAD)], 2)
            for i in range(QUAD):
                stage_in_ref[rows, tile(q * QUAD + i)] = out[i]
    for c0 in range(0, SSM_BLOCK_CHUNKS, rows16):
        rows = slice(c0, c0 + rows16)
        for q in range(n_quads):
            out = _swap_stage([stage_in_ref[rows, tile(q * QUAD + i)] for i in range(QUAD)], 1)
            for m in range(PAIRS_PER_SLAB):
                up_ref[m, rows, tile(q)] = out[m]

    for m in range(PAIRS_PER_SLAB):
        up = up_ref[m]
        s_ref[:, m * 2 * PAIR_STATES:(m + 1) * 2 * PAIR_STATES] = _dot(up, bst_ref[0, m])
        for nt in range(PAIR_IN // MXU_DIM):
            cols = slice(nt * MXU_DIM, (nt + 1) * MXU_DIM)
            k_hi = (nt + 1) * MXU_DIM
            yin_ref[m, :, cols] = _dot(up[:, 0:k_hi], toep_ref[0, m, 0:k_hi, cols])

    row = lax.broadcasted_iota(jnp.int32, (SUBLANES, PAIR_STATES), 0)
    for m in range(PAIRS_PER_SLAB):
        re = slice(m * 2 * PAIR_STATES, m * 2 * PAIR_STATES + PAIR_STATES)
        im = slice(m * 2 * PAIR_STATES + PAIR_STATES, (m + 1) * 2 * PAIR_STATES)
        bcast = lambda r, lanes: jnp.broadcast_to(pow_ref[r:r + 1, lanes], (SUBLANES, PAIR_STATES))
        steps = [(sh, bcast(r, re), bcast(r, im)) for sh, r in ((1, _ROW_A1), (2, _ROW_A2), (4, _ROW_A4))]
        pc_r = pow_ref[_ROW_CARRY:_ROW_CARRY + SUBLANES, re]
        pc_i = pow_ref[_ROW_CARRY:_ROW_CARRY + SUBLANES, im]
        cr = carry_ref[m, 0]
        ci = carry_ref[m, 1]
        for c0 in range(0, SSM_BLOCK_CHUNKS, rows16):
            prev = []
            for o in range(2):
                rows = slice(c0 + o * SUBLANES, c0 + (o + 1) * SUBLANES)
                er = s_ref[rows, re]
                ei = s_ref[rows, im]
                for sh, ar, ai in steps:
                    tr = jnp.where(row >= sh, pltpu.roll(er, sh, 0), 0.0)
                    ti = jnp.where(row >= sh, pltpu.roll(ei, sh, 0), 0.0)
                    er, ei = _cmul_add(ar, ai, tr, ti, er, ei)
                er, ei = _cmul_add(pc_r, pc_i, cr, ci, er, ei)
                prev.append((jnp.where(row == 0, cr, pltpu.roll(er, 1, 0)),
                             jnp.where(row == 0, ci, pltpu.roll(ei, 1, 0))))
                cr = jnp.broadcast_to(er[SUBLANES - 1:SUBLANES], (SUBLANES, PAIR_STATES))
                ci = jnp.broadcast_to(ei[SUBLANES - 1:SUBLANES], (SUBLANES, PAIR_STATES))
            rows = slice(c0, c0 + rows16)
            xprev_ref[rows, re] = jnp.concatenate([prev[0][0], prev[1][0]], axis=0).astype(bf16)
            xprev_ref[rows, im] = jnp.concatenate([prev[0][1], prev[1][1]], axis=0).astype(bf16)
        carry_ref[m, 0] = cr
        carry_ref[m, 1] = ci

    for m in range(PAIRS_PER_SLAB):
        yin_ref[m] += _dot_nt(xprev_ref[:, m * 2 * PAIR_STATES:(m + 1) * 2 * PAIR_STATES], cstt_ref[0, m])

    for c0 in range(0, SSM_BLOCK_CHUNKS, SUBLANES):
        rows = slice(c0, c0 + SUBLANES)
        for q in range(n_quads):
            out = _swap_stage([yin_ref[m, rows, tile(q)] for m in range(PAIRS_PER_SLAB)], 2)
            for m in range(PAIRS_PER_SLAB):
                y_ref[0, m, rows, tile(q)] = out[m]


def _ssm_core(u_rows, toep, bst, cstt, pow_tab):
    bsz, n_slabs, n_chunks, lanes = u_rows.shape
    seq = n_chunks * CHUNK
    assert seq % SSM_BLOCK_T == 0 and lanes == SLAB_IN
    bf16 = jnp.bfloat16
    f32 = jnp.float32
    tok_spec = pl.BlockSpec((1, 1, SSM_BLOCK_CHUNKS, SLAB_IN), lambda v, b, j: (b, v, j, 0))
    return pl.pallas_call(
        _ssm_core_kernel,
        grid=(N_SLABS, bsz, seq // SSM_BLOCK_T),
        in_specs=[tok_spec,
                  pl.BlockSpec((1,) + toep.shape[1:], lambda v, b, j: (v, 0, 0, 0)),
                  pl.BlockSpec((1,) + bst.shape[1:], lambda v, b, j: (v, 0, 0, 0)),
                  pl.BlockSpec((1,) + cstt.shape[1:], lambda v, b, j: (v, 0, 0, 0)),
                  pl.BlockSpec((_POW_ROWS, SLAB_STATES), lambda v, b, j: (0, v))],
        out_specs=pl.BlockSpec((1, PAIRS_PER_SLAB, SSM_BLOCK_CHUNKS, PAIR_IN), lambda v, b, j: (b, v, j, 0)),
        out_shape=jax.ShapeDtypeStruct((bsz, n_slabs * PAIRS_PER_SLAB, n_chunks, PAIR_IN), f32),
        scratch_shapes=[
            pltpu.VMEM((SSM_BLOCK_CHUNKS, SLAB_IN), bf16),
            pltpu.VMEM((PAIRS_PER_SLAB, SSM_BLOCK_CHUNKS, PAIR_IN), bf16),
            pltpu.VMEM((SSM_BLOCK_CHUNKS, SLAB_STATES), f32),
            pltpu.VMEM((SSM_BLOCK_CHUNKS, SLAB_STATES), bf16),
            pltpu.VMEM((PAIRS_PER_SLAB, SSM_BLOCK_CHUNKS, PAIR_IN), f32),
            pltpu.VMEM((PAIRS_PER_SLAB, 2, SUBLANES, PAIR_STATES), f32),
        ],
        compiler_params=pltpu.CompilerParams(
            dimension_semantics=("arbitrary", "arbitrary", "arbitrary"),
            vmem_limit_bytes=VMEM_LIMIT_BYTES),
        name="ssm_core",
    )(u_rows, toep, bst, cstt, pow_tab)


def _block_kernel(x_ref, mod_ref, npre_ref, npost_ref, win_ref, u_ref, yp_ref, dskip_ref, poolw_ref,
                  pscale_ref, gluw_ref, glub_ref, wbp_ref, wbs_ref, wout_ref, o_ref,
                  hist_ref, ytok_ref):
    j = pl.program_id(1)
    bf16 = jnp.bfloat16

    @pl.when(j == 0)
    def _():
        hist_ref[...] = jnp.zeros_like(hist_ref)

    gate = mod_ref[0, 2:3, :]
    for sub in range(BLOCK_T // SUB_T):
        r0 = sub * SUB_T
        x = x_ref[0, r0:r0 + SUB_T, :]
        hb = _rms_modulate(x, npre_ref[...], mod_ref[0, 1:2, :], mod_ref[0, 0:1, :]).astype(bf16)

        pn = _dot(hb, win_ref[:, 0:2 * D_MODEL])
        z_ssm = _dot(hb, win_ref[:, 3 * D_MODEL:4 * D_MODEL])
        gates = _dot(hb, win_ref[:, 4 * D_MODEL:6 * D_MODEL])
        u_pool = pn[:, 0:D_MODEL]

        pos = (j * BLOCK_T + r0 + 1
               + lax.broadcasted_iota(jnp.int32, (SUB_T, 1), 0)).astype(jnp.float32)
        mixed = []
        for g, w in enumerate(POOL_WINDOWS):
            cols = slice(g * POOL_GROUP_WIDTH, (g + 1) * POOL_GROUP_WIDTH)
            acc = jnp.concatenate([hist_ref[:, cols], u_pool[:, cols]], axis=0)
            span = 1
            while span < w:
                acc = acc + pltpu.roll(acc, span, 0)
                span *= 2
            inv_count = 1.0 / jnp.minimum(pos, float(w))
            pooled = acc[POOL_HALO:] * inv_count - u_pool[:, cols]
            mixed.append(_dot(pooled.astype(bf16), poolw_ref[g]))
        hist_ref[...] = u_pool[SUB_T - POOL_HALO:, :]
        y_pool = (jnp.concatenate(mixed, axis=-1) * pscale_ref[...]
                  * _silu_of_half(pn[:, D_MODEL:2 * D_MODEL]))

        chunk_rows = range(r0 // CHUNK, (r0 + SUB_T) // CHUNK, SUBLANES)
        tile = lambda k: slice(k * LANES, (k + 1) * LANES)
        for v in range(N_SLABS):
            pairs = [v * PAIRS_PER_SLAB + m for m in range(PAIRS_PER_SLAB)]
            for c0 in chunk_rows:
                rows = slice(c0, c0 + SUBLANES)
                for q in range(CHUNK // QUAD):
                    out = _swap_stage([yp_ref[0, p, rows, tile(q)] for p in pairs], 1)
                    for i in range(QUAD):
                        t = q * QUAD + i
                        ytok_ref[v, pl.ds(c0 * CHUNK + t, SUBLANES, stride=CHUNK), :] = (
                            out[i] + dskip_ref[v, :, tile(t)] * u_ref[0, v, rows, tile(t)])

        y = _gelu_tanh(jnp.concatenate([ytok_ref[v, r0:r0 + SUB_T, :] for v in range(N_SLABS)],
                                       axis=-1))
        y = y * _sigmoid_of_half(_dot(y.astype(bf16), gluw_ref[...]) + glub_ref[...])
        y_ssm = (y * _silu_of_half(z_ssm)).astype(bf16)

        merged = (_sigmoid_of_half(gates[:, 0:D_MODEL]) * _dot(y_pool.astype(bf16), wbp_ref[...])
                  + _sigmoid_of_half(gates[:, D_MODEL:2 * D_MODEL]) * _dot(y_ssm, wbs_ref[...]))
        out = _dot(merged.astype(bf16), wout_ref[...])
        rn = out * lax.rsqrt(jnp.mean(out * out, axis=-1, keepdims=True) + RMS_EPS) * npost_ref[...]
        o_ref[0, r0:r0 + SUB_T, :] = x + gate * rn


def _layer(x, c, w_ada, b_ada, norm_pre, norm_post, w_in, pool_w, pool_scale, a_re, a_im, log_dt,
           b_re, b_im, c_re, c_im, d_skip, glu_w, glu_b, w_branch_pool, w_branch_ssm, w_out):
    bsz, seq, d = x.shape
    assert d == D_MODEL and seq % BLOCK_T == 0
    bf16 = jnp.bfloat16
    f32 = jnp.float32

    toep, bst, cstt, pow_tab = _ssm_prep(a_re, a_im, log_dt, b_re, b_im, c_re, c_im)
    c_pad = jnp.zeros((SUBLANES, d), f32).at[:bsz].set(c)
    mod = _adaln(c_pad, w_ada, b_ada)[:bsz].reshape(bsz, 3, d)
    gate_cols = np.repeat(np.array([1.0, 0.5, 1.0, 0.5, 0.5, 0.5], np.float32), d)
    w_in_b = (w_in * gate_cols).astype(bf16)

    u_rows, u_rows_b = _ssm_in(x, mod, norm_pre, w_in_b[:, 2 * d:3 * d])
    y_pairs = _ssm_core(u_rows_b, toep, bst, cstt, pow_tab)
    d_rows = jnp.tile(d_skip.reshape(N_SLABS, 1, LANES), (1, SUBLANES, CHUNK))

    row = lambda v: v.reshape(1, d)
    tok_spec = pl.BlockSpec((1, BLOCK_T, d), lambda b, j: (b, j, 0))
    block_chunks = BLOCK_T // CHUNK
    u_spec = pl.BlockSpec((1, N_SLABS, block_chunks, SLAB_IN), lambda b, j: (b, 0, j, 0))
    y_spec = pl.BlockSpec((1, N_SLABS * PAIRS_PER_SLAB, block_chunks, PAIR_IN), lambda b, j: (b, 0, j, 0))
    operands = [
        (x, tok_spec),
        (mod, pl.BlockSpec((1, 3, d), lambda b, j: (b, 0, 0))),
        (row(norm_pre), None), (row(norm_post), None),
        (w_in_b, None),
        (u_rows, u_spec), (y_pairs, y_spec), (d_rows, None),
        (pool_w.astype(bf16), None), (row(pool_scale), None),
        ((0.5 * glu_w).astype(bf16), None), (row(0.5 * glu_b), None),
        (w_branch_pool.astype(bf16), None), (w_branch_ssm.astype(bf16), None),
        (w_out.astype(bf16), None),
    ]
    arrays = [a for a, _ in operands]
    specs = [s if s is not None else _const_spec(a.shape) for a, s in operands]

    return pl.pallas_call(
        _block_kernel,
        grid=(bsz, seq // BLOCK_T),
        in_specs=specs,
        out_specs=tok_spec,
        out_shape=jax.ShapeDtypeStruct((bsz, seq, d), x.dtype),
        scratch_shapes=[
            pltpu.VMEM((POOL_HALO, d), f32),
            pltpu.VMEM((N_SLABS, BLOCK_T, LANES), f32),
        ],
        compiler_params=pltpu.CompilerParams(
            dimension_semantics=("arbitrary", "arbitrary"),
            vmem_limit_bytes=VMEM_LIMIT_BYTES),
        name="block",
    )(*arrays)


def kernel(x, c, w_ada, b_ada, norm_pre, norm_post, w_in, pool_w, pool_scale, ssm_a_re, ssm_a_im,
           ssm_log_dt, ssm_b_re, ssm_b_im, ssm_c_re, ssm_c_im, ssm_d, glu_w, glu_b, w_branch_pool,
           w_branch_ssm, w_out):
    for layer in range(w_in.shape[0]):
        x = _layer(x, c, w_ada[layer], b_ada[layer], norm_pre[layer], norm_post[layer],
                   w_in[layer], pool_w[layer], pool_scale[layer], ssm_a_re[layer],
                   ssm_a_im[layer], ssm_log_dt[layer], ssm_b_re[layer], ssm_b_im[layer],
                   ssm_c_re[layer], ssm_c_im[layer], ssm_d.reshape(ssm_d.shape[0], -1)[layer],
                   glu_w[layer], glu_b[layer], w_branch_pool[layer], w_branch_ssm[layer],
                   w_out[layer])
    return x
```

```python
import math

import numpy as np
import jax
import jax.numpy as jnp
from jax import lax
from jax.experimental import pallas as pl
from jax.experimental.pallas import tpu as pltpu

D_MODEL = 1024
POOL_WINDOWS = (2, 4, 8, 16)
POOL_GROUP_WIDTH = D_MODEL // len(POOL_WINDOWS)
POOL_HALO = 16
SSM_GROUP = 16
SSM_GROUPS = D_MODEL // SSM_GROUP
SSM_STATE = 64
RMS_EPS = 1e-6

SUBLANES = 8
LANES = 128
MXU_DIM = 256
CHUNK = 16
GROUPS_PER_SLAB = LANES // SSM_GROUP
N_SLABS = D_MODEL // LANES
PAIRS_PER_SLAB = GROUPS_PER_SLAB // 2
PAIR_STATES = 2 * SSM_STATE
SLAB_IN = CHUNK * LANES
PAIR_LANES = 2 * SSM_GROUP
PAIR_IN = CHUNK * PAIR_LANES
QUAD = LANES // PAIR_LANES
SLAB_STATES = PAIRS_PER_SLAB * 2 * PAIR_STATES
BLOCK_T = 512
SUB_T = 256
SSM_IN_BLOCK_T = 1024
SSM_BLOCK_T = 8192
SSM_BLOCK_CHUNKS = SSM_BLOCK_T // CHUNK
VMEM_LIMIT_BYTES = 56 * 1024 * 1024

_ROW_A1, _ROW_A2, _ROW_A4, _ROW_CARRY = 0, 1, 2, 8
_POW_ROWS = 16


def _sigmoid(v):
    return 0.5 * jnp.tanh(0.5 * v) + 0.5


def _silu(v):
    return v * _sigmoid(v)


def _sigmoid_of_half(h):
    return 0.5 * jnp.tanh(h) + 0.5


def _silu_of_half(h):
    return h * jnp.tanh(h) + h


def _gelu_tanh(v):
    c = math.sqrt(2.0 / math.pi)
    return 0.5 * v * (1.0 + jnp.tanh(c * (v + 0.044715 * (v * v * v))))


def _dot(a, b):
    return jnp.dot(a, b, preferred_element_type=jnp.float32)


def _dot_nt(a, b, precision=None):
    return lax.dot_general(a, b, (((1,), (1,)), ((), ())), precision=precision,
                           preferred_element_type=jnp.float32)


def _cmul_add(ar, ai, xr, xi, br, bi):
    return ar * xr - ai * xi + br, ar * xi + ai * xr + bi


def _rms_modulate(x, norm_gain, scale, shift):
    xn = x * lax.rsqrt(jnp.mean(x * x, axis=-1, keepdims=True) + RMS_EPS) * norm_gain
    return xn * (1.0 + scale) + shift


def _const_spec(shape):
    nd = len(shape)
    return pl.BlockSpec(shape, lambda *_: (0,) * nd, pipeline_mode=pl.Buffered(1))


def _ssm_prep_kernel(are_ref, aim_ref, ldt_ref, bre_ref, bim_ref, cre_ref, cim_ref,
                     toep_ref, bst_ref, cstt_ref, pow_ref):
    bf16 = jnp.bfloat16
    in_group = [lax.broadcasted_iota(jnp.int32, (SSM_GROUP, LANES), 1) // SSM_STATE == g
                for g in range(2)]
    lane_block = lax.broadcasted_iota(jnp.int32, (SSM_GROUP, LANES), 1) // SSM_GROUP

    def split(v_re, v_im, g):
        return jnp.concatenate([jnp.where(in_group[g], v_re, 0.0), jnp.where(in_group[g], v_im, 0.0)],
                               axis=1)

    pow_ref[...] = jnp.zeros_like(pow_ref)
    for m in range(PAIRS_PER_SLAB):
        lanes = slice(m * LANES, (m + 1) * LANES)
        dt = jnp.exp(ldt_ref[:, lanes])
        lam_re = jnp.minimum(are_ref[:, lanes], -1e-4)
        lam_im = aim_ref[:, lanes]
        l_re = lam_re * dt
        l_im = lam_im * dt

        def cmul(p, q):
            return p[0] * q[0] - p[1] * q[1], p[0] * q[1] + p[1] * q[0]

        mag = jnp.exp(l_re)
        pows = [(jnp.ones_like(l_re), jnp.zeros_like(l_re)), (mag * jnp.cos(l_im), mag * jnp.sin(l_im))]
        for _ in range(CHUNK - 1):
            pows.append(cmul(pows[-1], pows[1]))
        abar_re, abar_im = pows[1]
        den = lam_re * lam_re + lam_im * lam_im
        num_re = abar_re - 1.0
        f_re = (num_re * lam_re + abar_im * lam_im) / den
        f_im = (abar_im * lam_re - num_re * lam_im) / den
        b_re = bre_ref[:, lanes]
        b_im = bim_ref[:, lanes]
        bb_re = f_re * b_re - f_im * b_im
        bb_im = f_re * b_im + f_im * b_re
        c_re = cre_ref[:, lanes]
        c_im = cim_ref[:, lanes]

        for s in range(CHUNK):
            ar, ai = pows[CHUNK - 1 - s]
            w_re = ar * bb_re - ai * bb_im
            w_im = ar * bb_im + ai * bb_re
            for g in range(2):
                r0 = s * PAIR_LANES + g * SSM_GROUP
                bst_ref[0, m, r0:r0 + SSM_GROUP, :] = split(w_re, w_im, g).astype(bf16)

        z = []
        for n in range(CHUNK + 1):
            ar, ai = pows[n]
            z.append((ar * c_re - ai * c_im, -(ar * c_im + ai * c_re)))
        for t in range(CHUNK):
            for g in range(2):
                r0 = t * PAIR_LANES + g * SSM_GROUP
                cstt_ref[0, m, r0:r0 + SSM_GROUP, :] = split(z[t + 1][0], z[t + 1][1], g).astype(bf16)

        z_all = jnp.concatenate([jnp.concatenate([z[n][0], z[n][1]], axis=1) for n in range(CHUNK)],
                                axis=0)
        lane_pair = lax.broadcasted_iota(jnp.int32, (SSM_GROUP, PAIR_IN), 1)
        for g in range(2):
            lag = _dot_nt(split(bb_re, bb_im, g), z_all, precision=lax.Precision.HIGHEST)
            cols = []
            for q in range(CHUNK // QUAD):
                col = jnp.zeros((SSM_GROUP, LANES), jnp.float32)
                for i in range(QUAD):
                    n = q * QUAD + i
                    src = lag[:, (n // GROUPS_PER_SLAB) * LANES:(n // GROUPS_PER_SLAB + 1) * LANES]
                    dst_block = 2 * i + g
                    shift = (SSM_GROUP * (dst_block - n % GROUPS_PER_SLAB)) % LANES
                    moved = pltpu.roll(src, shift, 1) if shift else src
                    col = jnp.where(lane_block == dst_block, moved, col)
                cols.append(col)
            lag_pair = jnp.concatenate(cols, axis=1)
            for s in range(CHUNK):
                r0 = s * PAIR_LANES + g * SSM_GROUP
                if s == 0:
                    blk = lag_pair
                else:
                    blk = jnp.where(lane_pair >= s * PAIR_LANES,
                                    pltpu.roll(lag_pair, s * PAIR_LANES, 1), 0.0)
                toep_ref[0, m, r0:r0 + SSM_GROUP, :] = blk.astype(bf16)

        def pow_row(p):
            return jnp.concatenate([p[0], p[1]], axis=1)

        cols = slice(m * 2 * PAIR_STATES, (m + 1) * 2 * PAIR_STATES)
        a16 = pows[CHUNK]
        a32 = cmul(a16, a16)
        pow_ref[_ROW_A1:_ROW_A1 + 1, cols] = pow_row(a16)
        pow_ref[_ROW_A2:_ROW_A2 + 1, cols] = pow_row(a32)
        pow_ref[_ROW_A4:_ROW_A4 + 1, cols] = pow_row(cmul(a32, a32))
        carry_pow = a16
        for r in range(SUBLANES):
            pow_ref[_ROW_CARRY + r:_ROW_CARRY + r + 1, cols] = pow_row(carry_pow)
            carry_pow = cmul(carry_pow, a16)


def _ssm_prep(a_re, a_im, log_dt, b_re, b_im, c_re, c_im):
    g, p, h = SSM_GROUPS, SSM_STATE, SSM_GROUP
    slab_lanes = GROUPS_PER_SLAB * p
    flat = lambda v: v.reshape(1, g * p)
    row_spec = pl.BlockSpec((1, slab_lanes), lambda v: (0, v))
    mat_spec = pl.BlockSpec((h, slab_lanes), lambda v: (0, v))
    bf16 = jnp.bfloat16
    states = (N_SLABS, PAIRS_PER_SLAB, PAIR_IN, 2 * PAIR_STATES)
    toep = (N_SLABS, PAIRS_PER_SLAB, PAIR_IN, PAIR_IN)
    return pl.pallas_call(
        _ssm_prep_kernel,
        grid=(N_SLABS,),
        in_specs=[row_spec, row_spec, row_spec, mat_spec, mat_spec, mat_spec, mat_spec],
        out_specs=(pl.BlockSpec((1,) + toep[1:], lambda v: (v, 0, 0, 0)),
                   pl.BlockSpec((1,) + states[1:], lambda v: (v, 0, 0, 0)),
                   pl.BlockSpec((1,) + states[1:], lambda v: (v, 0, 0, 0)),
                   pl.BlockSpec((_POW_ROWS, SLAB_STATES), lambda v: (0, v))),
        out_shape=(jax.ShapeDtypeStruct(toep, bf16),
                   jax.ShapeDtypeStruct(states, bf16),
                   jax.ShapeDtypeStruct(states, bf16),
                   jax.ShapeDtypeStruct((_POW_ROWS, N_SLABS * SLAB_STATES), jnp.float32)),
        name="ssm_prep",
    )(flat(a_re), flat(a_im), flat(jnp.repeat(log_dt, p)),
      b_re.reshape(g * p, h).T, b_im.reshape(g * p, h).T,
      c_re.transpose(1, 0, 2).reshape(h, g * p), c_im.transpose(1, 0, 2).reshape(h, g * p))


def _adaln_kernel(c_ref, w_ref, b_ref, o_ref):
    o_ref[...] = _dot(_silu(c_ref[...]), w_ref[...]) + b_ref[...]


def _adaln(c_pad, w_ada, b_ada):
    rows = c_pad.shape[0]
    return pl.pallas_call(
        _adaln_kernel,
        grid=(3,),
        in_specs=[pl.BlockSpec((rows, D_MODEL), lambda j: (0, 0)),
                  pl.BlockSpec((D_MODEL, D_MODEL), lambda j: (0, j)),
                  pl.BlockSpec((1, D_MODEL), lambda j: (0, j))],
        out_specs=pl.BlockSpec((rows, D_MODEL), lambda j: (0, j)),
        out_shape=jax.ShapeDtypeStruct((rows, 3 * D_MODEL), jnp.float32),
        name="adaln",
    )(c_pad, w_ada, b_ada.reshape(1, 3 * D_MODEL))


def _ssm_in_kernel(x_ref, mod_ref, npre_ref, perm_ref, wu_ref, u_ref, ub_ref):
    sub_chunks = SUB_T // CHUNK
    for sub in range(SSM_IN_BLOCK_T // SUB_T):
        r0 = sub * SUB_T
        hb = _rms_modulate(x_ref[0, r0:r0 + SUB_T, :], npre_ref[...], mod_ref[0, 1:2, :],
                           mod_ref[0, 0:1, :]).astype(jnp.bfloat16)
        hp = _dot(perm_ref[...], hb).astype(jnp.bfloat16)
        u = _dot(hp, wu_ref[...])
        for v in range(N_SLABS):
            for t in range(CHUNK):
                piece = u[t * sub_chunks:(t + 1) * sub_chunks, v * LANES:(v + 1) * LANES]
                dst = (0, v, slice(sub * sub_chunks, (sub + 1) * sub_chunks), slice(t * LANES, (t + 1) * LANES))
                u_ref[dst] = piece
                ub_ref[dst] = piece.astype(jnp.bfloat16)


def _chunk_permutation():
    n_chunks = SUB_T // CHUNK
    p = np.zeros((SUB_T, SUB_T), np.float32)
    for c in range(n_chunks):
        for t in range(CHUNK):
            p[t * n_chunks + c, c * CHUNK + t] = 1.0
    return p


def _ssm_in(x, mod, norm_pre, w_u):
    bsz, seq, d = x.shape
    rows_spec = pl.BlockSpec((1, N_SLABS, SSM_IN_BLOCK_T // CHUNK, SLAB_IN), lambda b, j: (b, 0, j, 0))
    rows_shape = (bsz, N_SLABS, seq // CHUNK, SLAB_IN)
    return pl.pallas_call(
        _ssm_in_kernel,
        grid=(bsz, seq // SSM_IN_BLOCK_T),
        in_specs=[pl.BlockSpec((1, SSM_IN_BLOCK_T, d), lambda b, j: (b, j, 0)),
                  pl.BlockSpec((1, 3, d), lambda b, j: (b, 0, 0)),
                  _const_spec((1, d)), _const_spec((SUB_T, SUB_T)), _const_spec((d, d))],
        out_specs=(rows_spec, rows_spec),
        out_shape=(jax.ShapeDtypeStruct(rows_shape, jnp.float32),
                   jax.ShapeDtypeStruct(rows_shape, jnp.bfloat16)),
        compiler_params=pltpu.CompilerParams(dimension_semantics=("arbitrary", "arbitrary"),
                                             vmem_limit_bytes=VMEM_LIMIT_BYTES),
        name="ssm_in",
    )(x, mod, norm_pre.reshape(1, d), jnp.asarray(_chunk_permutation(), jnp.bfloat16), w_u)


def _swap_stage(tiles, d):
    block = lax.broadcasted_iota(jnp.int32, tiles[0].shape, 1) // PAIR_LANES
    keep = (block & d) == 0
    out = list(tiles)
    for i in range(PAIRS_PER_SLAB):
        if i & d:
            continue
        a, b = tiles[i], tiles[i + d]
        out[i] = jnp.where(keep, a, pltpu.roll(b, d * PAIR_LANES, 1))
        out[i + d] = jnp.where(keep, pltpu.roll(a, LANES - d * PAIR_LANES, 1), b)
    return out


def _ssm_core_kernel(u_ref, toep_ref, bst_ref, cstt_ref, pow_ref, y_ref,
                     stage_in_ref, up_ref, s_ref, xprev_ref, yin_ref, carry_ref):
    bf16 = jnp.bfloat16
    n_quads = CHUNK // QUAD
    tile = lambda k: slice(k * LANES, (k + 1) * LANES)
    rows16 = 2 * SUBLANES

    @pl.when(pl.program_id(2) == 0)
    def _():
        carry_ref[...] = jnp.zeros_like(carry_ref)

    for c0 in range(0, SSM_BLOCK_CHUNKS, rows16):
        rows = slice(c0, c0 + rows16)
        for q in range(n_quads):
            out = _swap_stage([u_ref[0, 0, rows, tile(q * QUAD + i)] for i in range(QUAD)], 2)
            for i in range(QUAD):
                stage_in_ref[rows, tile(q * QUAD + i)] = out[i]
    for c0 in range(0, SSM_BLOCK_CHUNKS, rows16):
        rows = slice(c0, c0 + rows16)
        for q in range(n_quads):
            out = _swap_stage([stage_in_ref[rows, tile(q * QUAD + i)] for i in range(QUAD)], 1)
            for m in range(PAIRS_PER_SLAB):
                up_ref[m, rows, tile(q)] = out[m]

    for m in range(PAIRS_PER_SLAB):
        up = up_ref[m]
        s_ref[:, m * 2 * PAIR_STATES:(m + 1) * 2 * PAIR_STATES] = _dot(up, bst_ref[0, m])
        for nt in range(PAIR_IN // MXU_DIM):
            cols = slice(nt * MXU_DIM, (nt + 1) * MXU_DIM)
            k_hi = (nt + 1) * MXU_DIM
            yin_ref[m, :, cols] = _dot(up[:, 0:k_hi], toep_ref[0, m, 0:k_hi, cols])

    row = lax.broadcasted_iota(jnp.int32, (SUBLANES, PAIR_STATES), 0)
    for m in range(PAIRS_PER_SLAB):
        re = slice(m * 2 * PAIR_STATES, m * 2 * PAIR_STATES + PAIR_STATES)
        im = slice(m * 2 * PAIR_STATES + PAIR_STATES, (m + 1) * 2 * PAIR_STATES)
        bcast = lambda r, lanes: jnp.broadcast_to(pow_ref[r:r + 1, lanes], (SUBLANES, PAIR_STATES))
        steps = [(sh, bcast(r, re), bcast(r, im)) for sh, r in ((1, _ROW_A1), (2, _ROW_A2), (4, _ROW_A4))]
        pc_r = pow_ref[_ROW_CARRY:_ROW_CARRY + SUBLANES, re]
        pc_i = pow_ref[_ROW_CARRY:_ROW_CARRY + SUBLANES, im]
        cr = carry_ref[m, 0]
        ci = carry_ref[m, 1]
        for c0 in range(0, SSM_BLOCK_CHUNKS, rows16):
            prev = []
            for o in range(2):
                rows = slice(c0 + o * SUBLANES, c0 + (o + 1) * SUBLANES)
                er = s_ref[rows, re]
                ei = s_ref[rows, im]
                for sh, ar, ai in steps:
                    tr = jnp.where(row >= sh, pltpu.roll(er, sh, 0), 0.0)
                    ti = jnp.where(row >= sh, pltpu.roll(ei, sh, 0), 0.0)
                    er, ei = _cmul_add(ar, ai, tr, ti, er, ei)
                er, ei = _cmul_add(pc_r, pc_i, cr, ci, er, ei)
                prev.append((jnp.where(row == 0, cr, pltpu.roll(er, 1, 0)),
                             jnp.where(row == 0, ci, pltpu.roll(ei, 1, 0))))
                cr = jnp.broadcast_to(er[SUBLANES - 1:SUBLANES], (SUBLANES, PAIR_STATES))
                ci = jnp.broadcast_to(ei[SUBLANES - 1:SUBLANES], (SUBLANES, PAIR_STATES))
            rows = slice(c0, c0 + rows16)
            xprev_ref[rows, re] = jnp.concatenate([prev[0][0], prev[1][0]], axis=0).astype(bf16)
            xprev_ref[rows, im] = jnp.concatenate([prev[0][1], prev[1][1]], axis=0).astype(bf16)
        carry_ref[m, 0] = cr
        carry_ref[m, 1] = ci

    for m in range(PAIRS_PER_SLAB):
        y_ref[0, m] = yin_ref[m] + _dot_nt(xprev_ref[:, m * 2 * PAIR_STATES:(m + 1) * 2 * PAIR_STATES],
                                           cstt_ref[0, m])


def _ssm_core(u_rows, toep, bst, cstt, pow_tab):
    bsz, n_slabs, n_chunks, lanes = u_rows.shape
    seq = n_chunks * CHUNK
    assert seq % SSM_BLOCK_T == 0 and lanes == SLAB_IN
    bf16 = jnp.bfloat16
    f32 = jnp.float32
    tok_spec = pl.BlockSpec((1, 1, SSM_BLOCK_CHUNKS, SLAB_IN), lambda v, b, j: (b, v, j, 0))
    return pl.pallas_call(
        _ssm_core_kernel,
        grid=(N_SLABS, bsz, seq // SSM_BLOCK_T),
        in_specs=[tok_spec,
                  pl.BlockSpec((1,) + toep.shape[1:], lambda v, b, j: (v, 0, 0, 0)),
                  pl.BlockSpec((1,) + bst.shape[1:], lambda v, b, j: (v, 0, 0, 0)),
                  pl.BlockSpec((1,) + cstt.shape[1:], lambda v, b, j: (v, 0, 0, 0)),
                  pl.BlockSpec((_POW_ROWS, SLAB_STATES), lambda v, b, j: (0, v))],
        out_specs=pl.BlockSpec((1, PAIRS_PER_SLAB, SSM_BLOCK_CHUNKS, PAIR_IN), lambda v, b, j: (b, v, j, 0)),
        out_shape=jax.ShapeDtypeStruct((bsz, n_slabs * PAIRS_PER_SLAB, n_chunks, PAIR_IN), f32),
        scratch_shapes=[
            pltpu.VMEM((SSM_BLOCK_CHUNKS, SLAB_IN), bf16),
            pltpu.VMEM((PAIRS_PER_SLAB, SSM_BLOCK_CHUNKS, PAIR_IN), bf16),
            pltpu.VMEM((SSM_BLOCK_CHUNKS, SLAB_STATES), f32),
            pltpu.VMEM((SSM_BLOCK_CHUNKS, SLAB_STATES), bf16),
            pltpu.VMEM((PAIRS_PER_SLAB, SSM_BLOCK_CHUNKS, PAIR_IN), f32),
            pltpu.VMEM((PAIRS_PER_SLAB, 2, SUBLANES, PAIR_STATES), f32),
        ],
        compiler_params=pltpu.CompilerParams(
            dimension_semantics=("arbitrary", "arbitrary", "arbitrary"),
            vmem_limit_bytes=VMEM_LIMIT_BYTES),
        name="ssm_core",
    )(u_rows, toep, bst, cstt, pow_tab)


def _block_kernel(x_ref, mod_ref, npre_ref, npost_ref, win_ref, u_ref, yp_ref, dskip_ref, poolw_ref,
                  pscale_ref, gluw_ref, glub_ref, wbp_ref, wbs_ref, wout_ref, o_ref,
                  hist_ref, stage_ref, ytok_ref):
    j = pl.program_id(1)
    bf16 = jnp.bfloat16

    @pl.when(j == 0)
    def _():
        hist_ref[...] = jnp.zeros_like(hist_ref)

    gate = mod_ref[0, 2:3, :]
    for sub in range(BLOCK_T // SUB_T):
        r0 = sub * SUB_T
        x = x_ref[0, r0:r0 + SUB_T, :]
        hb = _rms_modulate(x, npre_ref[...], mod_ref[0, 1:2, :], mod_ref[0, 0:1, :]).astype(bf16)

        pn = _dot(hb, win_ref[:, 0:2 * D_MODEL])
        z_ssm = _dot(hb, win_ref[:, 3 * D_MODEL:4 * D_MODEL])
        gates = _dot(hb, win_ref[:, 4 * D_MODEL:6 * D_MODEL])
        u_pool = pn[:, 0:D_MODEL]

        pos = (j * BLOCK_T + r0 + 1
               + lax.broadcasted_iota(jnp.int32, (SUB_T, 1), 0)).astype(jnp.float32)
        mixed = []
        for g, w in enumerate(POOL_WINDOWS):
            cols = slice(g * POOL_GROUP_WIDTH, (g + 1) * POOL_GROUP_WIDTH)
            acc = jnp.concatenate([hist_ref[:, cols], u_pool[:, cols]], axis=0)
            span = 1
            while span < w:
                acc = acc + pltpu.roll(acc, span, 0)
                span *= 2
            inv_count = 1.0 / jnp.minimum(pos, float(w))
            pooled = acc[POOL_HALO:] * inv_count - u_pool[:, cols]
            mixed.append(_dot(pooled.astype(bf16), poolw_ref[g]))
        hist_ref[...] = u_pool[SUB_T - POOL_HALO:, :]
        y_pool = (jnp.concatenate(mixed, axis=-1) * pscale_ref[...]
                  * _silu_of_half(pn[:, D_MODEL:2 * D_MODEL]))

        chunk_rows = range(r0 // CHUNK, (r0 + SUB_T) // CHUNK, SUBLANES)
        tile = lambda k: slice(k * LANES, (k + 1) * LANES)
        for v in range(N_SLABS):
            pairs = [v * PAIRS_PER_SLAB + m for m in range(PAIRS_PER_SLAB)]
            for c0 in chunk_rows:
                rows = slice(c0, c0 + SUBLANES)
                for q in range(CHUNK // QUAD):
                    out = _swap_stage([yp_ref[0, p, rows, tile(q)] for p in pairs], 2)
                    for p, o in zip(pairs, out):
                        stage_ref[p, rows, tile(q)] = o
        for v in range(N_SLABS):
            pairs = [v * PAIRS_PER_SLAB + m for m in range(PAIRS_PER_SLAB)]
            for c0 in chunk_rows:
                rows = slice(c0, c0 + SUBLANES)
                for q in range(CHUNK // QUAD):
                    out = _swap_stage([stage_ref[p, rows, tile(q)] for p in pairs], 1)
                    for i in range(QUAD):
                        t = q * QUAD + i
                        ytok_ref[v, pl.ds(c0 * CHUNK + t, SUBLANES, stride=CHUNK), :] = (
                            out[i] + dskip_ref[v, :, tile(t)] * u_ref[0, v, rows, tile(t)])

        y = _gelu_tanh(jnp.concatenate([ytok_ref[v, r0:r0 + SUB_T, :] for v in range(N_SLABS)],
                                       axis=-1))
        y = y * _sigmoid_of_half(_dot(y.astype(bf16), gluw_ref[...]) + glub_ref[...])
        y_ssm = (y * _silu_of_half(z_ssm)).astype(bf16)

        merged = (_sigmoid_of_half(gates[:, 0:D_MODEL]) * _dot(y_pool.astype(bf16), wbp_ref[...])
                  + _sigmoid_of_half(gates[:, D_MODEL:2 * D_MODEL]) * _dot(y_ssm, wbs_ref[...]))
        out = _dot(merged.astype(bf16), wout_ref[...])
        rn = out * lax.rsqrt(jnp.mean(out * out, axis=-1, keepdims=True) + RMS_EPS) * npost_ref[...]
        o_ref[0, r0:r0 + SUB_T, :] = x + gate * rn


def _layer(x, c, w_ada, b_ada, norm_pre, norm_post, w_in, pool_w, pool_scale, a_re, a_im, log_dt,
           b_re, b_im, c_re, c_im, d_skip, glu_w, glu_b, w_branch_pool, w_branch_ssm, w_out):
    bsz, seq, d = x.shape
    assert d == D_MODEL and seq % BLOCK_T == 0
    bf16 = jnp.bfloat16
    f32 = jnp.float32

    toep, bst, cstt, pow_tab = _ssm_prep(a_re, a_im, log_dt, b_re, b_im, c_re, c_im)
    c_pad = jnp.zeros((SUBLANES, d), f32).at[:bsz].set(c)
    mod = _adaln(c_pad, w_ada, b_ada)[:bsz].reshape(bsz, 3, d)
    gate_cols = np.repeat(np.array([1.0, 0.5, 1.0, 0.5, 0.5, 0.5], np.float32), d)
    w_in_b = (w_in * gate_cols).astype(bf16)

    u_rows, u_rows_b = _ssm_in(x, mod, norm_pre, w_in_b[:, 2 * d:3 * d])
    y_pairs = _ssm_core(u_rows_b, toep, bst, cstt, pow_tab)
    d_rows = jnp.tile(d_skip.reshape(N_SLABS, 1, LANES), (1, SUBLANES, CHUNK))

    row = lambda v: v.reshape(1, d)
    tok_spec = pl.BlockSpec((1, BLOCK_T, d), lambda b, j: (b, j, 0))
    block_chunks = BLOCK_T // CHUNK
    u_spec = pl.BlockSpec((1, N_SLABS, block_chunks, SLAB_IN), lambda b, j: (b, 0, j, 0))
    y_spec = pl.BlockSpec((1, N_SLABS * PAIRS_PER_SLAB, block_chunks, PAIR_IN), lambda b, j: (b, 0, j, 0))
    operands = [
        (x, tok_spec),
        (mod, pl.BlockSpec((1, 3, d), lambda b, j: (b, 0, 0))),
        (row(norm_pre), None), (row(norm_post), None),
        (w_in_b, None),
        (u_rows, u_spec), (y_pairs, y_spec), (d_rows, None),
        (pool_w.astype(bf16), None), (row(pool_scale), None),
        ((0.5 * glu_w).astype(bf16), None), (row(0.5 * glu_b), None),
        (w_branch_pool.astype(bf16), None), (w_branch_ssm.astype(bf16), None),
        (w_out.astype(bf16), None),
    ]
    arrays = [a for a, _ in operands]
    specs = [s if s is not None else _const_spec(a.shape) for a, s in operands]

    return pl.pallas_call(
        _block_kernel,
        grid=(bsz, seq // BLOCK_T),
        in_specs=specs,
        out_specs=tok_spec,
        out_shape=jax.ShapeDtypeStruct((bsz, seq, d), x.dtype),
        scratch_shapes=[
            pltpu.VMEM((POOL_HALO, d), f32),
            pltpu.VMEM((N_SLABS * PAIRS_PER_SLAB, BLOCK_T // CHUNK, PAIR_IN), f32),
            pltpu.VMEM((N_SLABS, BLOCK_T, LANES), f32),
        ],
        compiler_params=pltpu.CompilerParams(
            dimension_semantics=("arbitrary", "arbitrary"),
            vmem_limit_bytes=VMEM_LIMIT_BYTES),
        name="block",
    )(*arrays)


def kernel(x, c, w_ada, b_ada, norm_pre, norm_post, w_in, pool_w, pool_scale, ssm_a_re, ssm_a_im,
           ssm_log_dt, ssm_b_re, ssm_b_im, ssm_c_re, ssm_c_im, ssm_d, glu_w, glu_b, w_branch_pool,
           w_branch_ssm, w_out):
    for layer in range(w_in.shape[0]):
        x = _layer(x, c, w_ada[layer], b_ada[layer], norm_pre[layer], norm_post[layer],
                   w_in[layer], pool_w[layer], pool_scale[layer], ssm_a_re[layer],
                   ssm_a_im[layer], ssm_log_dt[layer], ssm_b_re[layer], ssm_b_im[layer],
                   ssm_c_re[layer], ssm_c_im[layer], ssm_d.reshape(ssm_d.shape[0], -1)[layer],
                   glu_w[layer], glu_b[layer], w_branch_pool[layer], w_branch_ssm[layer],
                   w_out[layer])
    return x
```

```python
import math

import numpy as np
import jax
import jax.numpy as jnp
from jax import lax
from jax.experimental import pallas as pl
from jax.experimental.pallas import tpu as pltpu

D_MODEL = 1024
POOL_WINDOWS = (2, 4, 8, 16)
POOL_GROUP_WIDTH = D_MODEL // len(POOL_WINDOWS)
POOL_HALO = 16
SSM_GROUP = 16
SSM_GROUPS = D_MODEL // SSM_GROUP
SSM_STATE = 64
RMS_EPS = 1e-6

SUBLANES = 8
LANES = 128
MXU_DIM = 256
CHUNK = 16
GROUPS_PER_SLAB = LANES // SSM_GROUP
N_SLABS = D_MODEL // LANES
PAIRS_PER_SLAB = GROUPS_PER_SLAB // 2
PAIR_STATES = 2 * SSM_STATE
SLAB_IN = CHUNK * LANES
PAIR_LANES = 2 * SSM_GROUP
PAIR_IN = CHUNK * PAIR_LANES
QUAD = LANES // PAIR_LANES
SLAB_STATES = PAIRS_PER_SLAB * 2 * PAIR_STATES
BLOCK_T = 512
SUB_T = 256
SSM_IN_BLOCK_T = 1024
SSM_BLOCK_T = 8192
SSM_BLOCK_CHUNKS = SSM_BLOCK_T // CHUNK
VMEM_LIMIT_BYTES = 56 * 1024 * 1024

_ROW_A1, _ROW_A2, _ROW_A4, _ROW_CARRY = 0, 1, 2, 8
_POW_ROWS = 16


def _sigmoid(v):
    return 0.5 * jnp.tanh(0.5 * v) + 0.5


def _silu(v):
    return v * _sigmoid(v)


def _sigmoid_of_half(h):
    return 0.5 * jnp.tanh(h) + 0.5


def _silu_of_half(h):
    return h * jnp.tanh(h) + h


def _gelu_tanh(v):
    c = math.sqrt(2.0 / math.pi)
    return 0.5 * v * (1.0 + jnp.tanh(c * (v + 0.044715 * (v * v * v))))


def _dot(a, b):
    return jnp.dot(a, b, preferred_element_type=jnp.float32)


def _dot_nt(a, b, precision=None):
    return lax.dot_general(a, b, (((1,), (1,)), ((), ())), precision=precision,
                           preferred_element_type=jnp.float32)


def _cmul_add(ar, ai, xr, xi, br, bi):
    return ar * xr - ai * xi + br, ar * xi + ai * xr + bi


def _rms_modulate(x, norm_gain, scale, shift):
    xn = x * lax.rsqrt(jnp.mean(x * x, axis=-1, keepdims=True) + RMS_EPS) * norm_gain
    return xn * (1.0 + scale) + shift


def _const_spec(shape):
    nd = len(shape)
    return pl.BlockSpec(shape, lambda *_: (0,) * nd, pipeline_mode=pl.Buffered(1))


def _ssm_prep_kernel(are_ref, aim_ref, ldt_ref, bre_ref, bim_ref, cre_ref, cim_ref,
                     toep_ref, bst_ref, cstt_ref, pow_ref):
    bf16 = jnp.bfloat16
    in_group = [lax.broadcasted_iota(jnp.int32, (SSM_GROUP, LANES), 1) // SSM_STATE == g
                for g in range(2)]
    lane_block = lax.broadcasted_iota(jnp.int32, (SSM_GROUP, LANES), 1) // SSM_GROUP

    def split(v_re, v_im, g):
        return jnp.concatenate([jnp.where(in_group[g], v_re, 0.0), jnp.where(in_group[g], v_im, 0.0)],
                               axis=1)

    pow_ref[...] = jnp.zeros_like(pow_ref)
    for m in range(PAIRS_PER_SLAB):
        lanes = slice(m * LANES, (m + 1) * LANES)
        dt = jnp.exp(ldt_ref[:, lanes])
        lam_re = jnp.minimum(are_ref[:, lanes], -1e-4)
        lam_im = aim_ref[:, lanes]
        l_re = lam_re * dt
        l_im = lam_im * dt

        def cmul(p, q):
            return p[0] * q[0] - p[1] * q[1], p[0] * q[1] + p[1] * q[0]

        mag = jnp.exp(l_re)
        pows = [(jnp.ones_like(l_re), jnp.zeros_like(l_re)), (mag * jnp.cos(l_im), mag * jnp.sin(l_im))]
        for _ in range(CHUNK - 1):
            pows.append(cmul(pows[-1], pows[1]))
        abar_re, abar_im = pows[1]
        den = lam_re * lam_re + lam_im * lam_im
        num_re = abar_re - 1.0
        f_re = (num_re * lam_re + abar_im * lam_im) / den
        f_im = (abar_im * lam_re - num_re * lam_im) / den
        b_re = bre_ref[:, lanes]
        b_im = bim_ref[:, lanes]
        bb_re = f_re * b_re - f_im * b_im
        bb_im = f_re * b_im + f_im * b_re
        c_re = cre_ref[:, lanes]
        c_im = cim_ref[:, lanes]

        for s in range(CHUNK):
            ar, ai = pows[CHUNK - 1 - s]
            w_re = ar * bb_re - ai * bb_im
            w_im = ar * bb_im + ai * bb_re
            for g in range(2):
                r0 = s * PAIR_LANES + g * SSM_GROUP
                bst_ref[0, m, r0:r0 + SSM_GROUP, :] = split(w_re, w_im, g).astype(bf16)

        z = []
        for n in range(CHUNK + 1):
            ar, ai = pows[n]
            z.append((ar * c_re - ai * c_im, -(ar * c_im + ai * c_re)))
        for t in range(CHUNK):
            for g in range(2):
                r0 = t * PAIR_LANES + g * SSM_GROUP
                cstt_ref[0, m, r0:r0 + SSM_GROUP, :] = split(z[t + 1][0], z[t + 1][1], g).astype(bf16)

        z_all = jnp.concatenate([jnp.concatenate([z[n][0], z[n][1]], axis=1) for n in range(CHUNK)],
                                axis=0)
        lane_pair = lax.broadcasted_iota(jnp.int32, (SSM_GROUP, PAIR_IN), 1)
        for g in range(2):
            lag = _dot_nt(split(bb_re, bb_im, g), z_all, precision=lax.Precision.HIGHEST)
            cols = []
            for q in range(CHUNK // QUAD):
                col = jnp.zeros((SSM_GROUP, LANES), jnp.float32)
                for i in range(QUAD):
                    n = q * QUAD + i
                    src = lag[:, (n // GROUPS_PER_SLAB) * LANES:(n // GROUPS_PER_SLAB + 1) * LANES]
                    dst_block = 2 * i + g
                    shift = (SSM_GROUP * (dst_block - n % GROUPS_PER_SLAB)) % LANES
                    moved = pltpu.roll(src, shift, 1) if shift else src
                    col = jnp.where(lane_block == dst_block, moved, col)
                cols.append(col)
            lag_pair = jnp.concatenate(cols, axis=1)
            for s in range(CHUNK):
                r0 = s * PAIR_LANES + g * SSM_GROUP
                if s == 0:
                    blk = lag_pair
                else:
                    blk = jnp.where(lane_pair >= s * PAIR_LANES,
                                    pltpu.roll(lag_pair, s * PAIR_LANES, 1), 0.0)
                toep_ref[0, m, r0:r0 + SSM_GROUP, :] = blk.astype(bf16)

        def pow_row(p):
            return jnp.concatenate([p[0], p[1]], axis=1)

        cols = slice(m * 2 * PAIR_STATES, (m + 1) * 2 * PAIR_STATES)
        a16 = pows[CHUNK]
        a32 = cmul(a16, a16)
        pow_ref[_ROW_A1:_ROW_A1 + 1, cols] = pow_row(a16)
        pow_ref[_ROW_A2:_ROW_A2 + 1, cols] = pow_row(a32)
        pow_ref[_ROW_A4:_ROW_A4 + 1, cols] = pow_row(cmul(a32, a32))
        carry_pow = a16
        for r in range(SUBLANES):
            pow_ref[_ROW_CARRY + r:_ROW_CARRY + r + 1, cols] = pow_row(carry_pow)
            carry_pow = cmul(carry_pow, a16)


def _ssm_prep(a_re, a_im, log_dt, b_re, b_im, c_re, c_im):
    g, p, h = SSM_GROUPS, SSM_STATE, SSM_GROUP
    slab_lanes = GROUPS_PER_SLAB * p
    flat = lambda v: v.reshape(1, g * p)
    row_spec = pl.BlockSpec((1, slab_lanes), lambda v: (0, v))
    mat_spec = pl.BlockSpec((h, slab_lanes), lambda v: (0, v))
    bf16 = jnp.bfloat16
    states = (N_SLABS, PAIRS_PER_SLAB, PAIR_IN, 2 * PAIR_STATES)
    toep = (N_SLABS, PAIRS_PER_SLAB, PAIR_IN, PAIR_IN)
    return pl.pallas_call(
        _ssm_prep_kernel,
        grid=(N_SLABS,),
        in_specs=[row_spec, row_spec, row_spec, mat_spec, mat_spec, mat_spec, mat_spec],
        out_specs=(pl.BlockSpec((1,) + toep[1:], lambda v: (v, 0, 0, 0)),
                   pl.BlockSpec((1,) + states[1:], lambda v: (v, 0, 0, 0)),
                   pl.BlockSpec((1,) + states[1:], lambda v: (v, 0, 0, 0)),
                   pl.BlockSpec((_POW_ROWS, SLAB_STATES), lambda v: (0, v))),
        out_shape=(jax.ShapeDtypeStruct(toep, bf16),
                   jax.ShapeDtypeStruct(states, bf16),
                   jax.ShapeDtypeStruct(states, bf16),
                   jax.ShapeDtypeStruct((_POW_ROWS, N_SLABS * SLAB_STATES), jnp.float32)),
        name="ssm_prep",
    )(flat(a_re), flat(a_im), flat(jnp.repeat(log_dt, p)),
      b_re.reshape(g * p, h).T, b_im.reshape(g * p, h).T,
      c_re.transpose(1, 0, 2).reshape(h, g * p), c_im.transpose(1, 0, 2).reshape(h, g * p))


def _adaln_kernel(c_ref, w_ref, b_ref, o_ref):
    o_ref[...] = _dot(_silu(c_ref[...]), w_ref[...]) + b_ref[...]


def _adaln(c_pad, w_ada, b_ada):
    rows = c_pad.shape[0]
    return pl.pallas_call(
        _adaln_kernel,
        grid=(3,),
        in_specs=[pl.BlockSpec((rows, D_MODEL), lambda j: (0, 0)),
                  pl.BlockSpec((D_MODEL, D_MODEL), lambda j: (0, j)),
                  pl.BlockSpec((1, D_MODEL), lambda j: (0, j))],
        out_specs=pl.BlockSpec((rows, D_MODEL), lambda j: (0, j)),
        out_shape=jax.ShapeDtypeStruct((rows, 3 * D_MODEL), jnp.float32),
        name="adaln",
    )(c_pad, w_ada, b_ada.reshape(1, 3 * D_MODEL))


def _swap_stage(tiles, d):
    block = lax.broadcasted_iota(jnp.int32, tiles[0].shape, 1) // PAIR_LANES
    keep = (block & d) == 0
    out = list(tiles)
    for i in range(PAIRS_PER_SLAB):
        if i & d:
            continue
        a, b = tiles[i], tiles[i + d]
        out[i] = jnp.where(keep, a, pltpu.roll(b, d * PAIR_LANES, 1))
        out[i + d] = jnp.where(keep, pltpu.roll(a, LANES - d * PAIR_LANES, 1), b)
    return out


def _ssm_in_kernel(x_ref, mod_ref, npre_ref, perm_ref, wu_ref, u_ref, up_ref, stage_ref):
    bf16 = jnp.bfloat16
    sub_chunks = SUB_T // CHUNK
    tile = lambda k: slice(k * LANES, (k + 1) * LANES)
    for sub in range(SSM_IN_BLOCK_T // SUB_T):
        r0 = sub * SUB_T
        chunks = slice(sub * sub_chunks, (sub + 1) * sub_chunks)
        hb = _rms_modulate(x_ref[0, r0:r0 + SUB_T, :], npre_ref[...], mod_ref[0, 1:2, :],
                           mod_ref[0, 0:1, :]).astype(bf16)
        hp = _dot(perm_ref[...], hb).astype(bf16)
        u = _dot(hp, wu_ref[...])
        token_rows = lambda t: slice(t * sub_chunks, (t + 1) * sub_chunks)
        for v in range(N_SLABS):
            for t in range(CHUNK):
                u_ref[0, v, chunks, tile(t)] = u[token_rows(t), tile(v)]
        for v in range(N_SLABS):
            for q in range(CHUNK // QUAD):
                out = _swap_stage([u[token_rows(q * QUAD + i), tile(v)].astype(bf16) for i in range(QUAD)], 2)
                for i in range(QUAD):
                    stage_ref[r0 + (q * QUAD + i) * sub_chunks:r0 + (q * QUAD + i + 1) * sub_chunks, tile(v)] = out[i]
        for v in range(N_SLABS):
            for q in range(CHUNK // QUAD):
                out = _swap_stage([stage_ref[r0 + (q * QUAD + i) * sub_chunks:r0 + (q * QUAD + i + 1) * sub_chunks,
                                             tile(v)] for i in range(QUAD)], 1)
                for m in range(PAIRS_PER_SLAB):
                    up_ref[0, v * PAIRS_PER_SLAB + m, chunks, tile(q)] = out[m]


def _chunk_permutation():
    n_chunks = SUB_T // CHUNK
    p = np.zeros((SUB_T, SUB_T), np.float32)
    for c in range(n_chunks):
        for t in range(CHUNK):
            p[t * n_chunks + c, c * CHUNK + t] = 1.0
    return p


def _ssm_in(x, mod, norm_pre, w_u):
    bsz, seq, d = x.shape
    block_chunks = SSM_IN_BLOCK_T // CHUNK
    n_pairs = N_SLABS * PAIRS_PER_SLAB
    return pl.pallas_call(
        _ssm_in_kernel,
        grid=(bsz, seq // SSM_IN_BLOCK_T),
        in_specs=[pl.BlockSpec((1, SSM_IN_BLOCK_T, d), lambda b, j: (b, j, 0)),
                  pl.BlockSpec((1, 3, d), lambda b, j: (b, 0, 0)),
                  _const_spec((1, d)), _const_spec((SUB_T, SUB_T)), _const_spec((d, d))],
        out_specs=(pl.BlockSpec((1, N_SLABS, block_chunks, SLAB_IN), lambda b, j: (b, 0, j, 0)),
                   pl.BlockSpec((1, n_pairs, block_chunks, PAIR_IN), lambda b, j: (b, 0, j, 0))),
        out_shape=(jax.ShapeDtypeStruct((bsz, N_SLABS, seq // CHUNK, SLAB_IN), jnp.float32),
                   jax.ShapeDtypeStruct((bsz, n_pairs, seq // CHUNK, PAIR_IN), jnp.bfloat16)),
        scratch_shapes=[pltpu.VMEM((SSM_IN_BLOCK_T, D_MODEL), jnp.bfloat16)],
        compiler_params=pltpu.CompilerParams(dimension_semantics=("arbitrary", "arbitrary"),
                                             vmem_limit_bytes=VMEM_LIMIT_BYTES),
        name="ssm_in",
    )(x, mod, norm_pre.reshape(1, d), jnp.asarray(_chunk_permutation(), jnp.bfloat16), w_u)


def _ssm_core_kernel(u_ref, toep_ref, bst_ref, cstt_ref, pow_ref, y_ref,
                     s_ref, xprev_ref, yin_ref, carry_ref):
    bf16 = jnp.bfloat16
    rows16 = 2 * SUBLANES

    @pl.when(pl.program_id(2) == 0)
    def _():
        carry_ref[...] = jnp.zeros_like(carry_ref)

    for m in range(PAIRS_PER_SLAB):
        up = u_ref[0, m]
        s_ref[:, m * 2 * PAIR_STATES:(m + 1) * 2 * PAIR_STATES] = _dot(up, bst_ref[0, m])
        for nt in range(PAIR_IN // MXU_DIM):
            cols = slice(nt * MXU_DIM, (nt + 1) * MXU_DIM)
            k_hi = (nt + 1) * MXU_DIM
            yin_ref[m, :, cols] = _dot(up[:, 0:k_hi], toep_ref[0, m, 0:k_hi, cols])

    row = lax.broadcasted_iota(jnp.int32, (SUBLANES, PAIR_STATES), 0)
    for m in range(PAIRS_PER_SLAB):
        re = slice(m * 2 * PAIR_STATES, m * 2 * PAIR_STATES + PAIR_STATES)
        im = slice(m * 2 * PAIR_STATES + PAIR_STATES, (m + 1) * 2 * PAIR_STATES)
        bcast = lambda r, lanes: jnp.broadcast_to(pow_ref[r:r + 1, lanes], (SUBLANES, PAIR_STATES))
        steps = [(sh, bcast(r, re), bcast(r, im)) for sh, r in ((1, _ROW_A1), (2, _ROW_A2), (4, _ROW_A4))]
        pc_r = pow_ref[_ROW_CARRY:_ROW_CARRY + SUBLANES, re]
        pc_i = pow_ref[_ROW_CARRY:_ROW_CARRY + SUBLANES, im]
        cr = carry_ref[m, 0]
        ci = carry_ref[m, 1]
        for c0 in range(0, SSM_BLOCK_CHUNKS, rows16):
            prev = []
            for o in range(2):
                rows = slice(c0 + o * SUBLANES, c0 + (o + 1) * SUBLANES)
                er = s_ref[rows, re]
                ei = s_ref[rows, im]
                for sh, ar, ai in steps:
                    tr = jnp.where(row >= sh, pltpu.roll(er, sh, 0), 0.0)
                    ti = jnp.where(row >= sh, pltpu.roll(ei, sh, 0), 0.0)
                    er, ei = _cmul_add(ar, ai, tr, ti, er, ei)
                er, ei = _cmul_add(pc_r, pc_i, cr, ci, er, ei)
                prev.append((jnp.where(row == 0, cr, pltpu.roll(er, 1, 0)),
                             jnp.where(row == 0, ci, pltpu.roll(ei, 1, 0))))
                cr = jnp.broadcast_to(er[SUBLANES - 1:SUBLANES], (SUBLANES, PAIR_STATES))
                ci = jnp.broadcast_to(ei[SUBLANES - 1:SUBLANES], (SUBLANES, PAIR_STATES))
            rows = slice(c0, c0 + rows16)
            xprev_ref[rows, re] = jnp.concatenate([prev[0][0], prev[1][0]], axis=0).astype(bf16)
            xprev_ref[rows, im] = jnp.concatenate([prev[0][1], prev[1][1]], axis=0).astype(bf16)
        carry_ref[m, 0] = cr
        carry_ref[m, 1] = ci

    for m in range(PAIRS_PER_SLAB):
        y_ref[0, m] = yin_ref[m] + _dot_nt(xprev_ref[:, m * 2 * PAIR_STATES:(m + 1) * 2 * PAIR_STATES],
                                           cstt_ref[0, m])


def _ssm_core(u_pairs, toep, bst, cstt, pow_tab):
    bsz, n_pairs, n_chunks, lanes = u_pairs.shape
    seq = n_chunks * CHUNK
    assert seq % SSM_BLOCK_T == 0 and lanes == PAIR_IN
    bf16 = jnp.bfloat16
    f32 = jnp.float32
    tok_spec = pl.BlockSpec((1, PAIRS_PER_SLAB, SSM_BLOCK_CHUNKS, PAIR_IN), lambda v, b, j: (b, v, j, 0))
    return pl.pallas_call(
        _ssm_core_kernel,
        grid=(N_SLABS, bsz, seq // SSM_BLOCK_T),
        in_specs=[tok_spec,
                  pl.BlockSpec((1,) + toep.shape[1:], lambda v, b, j: (v, 0, 0, 0)),
                  pl.BlockSpec((1,) + bst.shape[1:], lambda v, b, j: (v, 0, 0, 0)),
                  pl.BlockSpec((1,) + cstt.shape[1:], lambda v, b, j: (v, 0, 0, 0)),
                  pl.BlockSpec((_POW_ROWS, SLAB_STATES), lambda v, b, j: (0, v))],
        out_specs=tok_spec,
        out_shape=jax.ShapeDtypeStruct(u_pairs.shape, f32),
        scratch_shapes=[
            pltpu.VMEM((SSM_BLOCK_CHUNKS, SLAB_STATES), f32),
            pltpu.VMEM((SSM_BLOCK_CHUNKS, SLAB_STATES), bf16),
            pltpu.VMEM((PAIRS_PER_SLAB, SSM_BLOCK_CHUNKS, PAIR_IN), f32),
            pltpu.VMEM((PAIRS_PER_SLAB, 2, SUBLANES, PAIR_STATES), f32),
        ],
        compiler_params=pltpu.CompilerParams(
            dimension_semantics=("arbitrary", "arbitrary", "arbitrary"),
            vmem_limit_bytes=VMEM_LIMIT_BYTES),
        name="ssm_core",
    )(u_pairs, toep, bst, cstt, pow_tab)


def _block_kernel(x_ref, mod_ref, npre_ref, npost_ref, win_ref, u_ref, yp_ref, dskip_ref, poolw_ref,
                  pscale_ref, gluw_ref, glub_ref, wbp_ref, wbs_ref, wout_ref, o_ref,
                  hist_ref, stage_ref, ytok_ref):
    j = pl.program_id(1)
    bf16 = jnp.bfloat16

    @pl.when(j == 0)
    def _():
        hist_ref[...] = jnp.zeros_like(hist_ref)

    gate = mod_ref[0, 2:3, :]
    for sub in range(BLOCK_T // SUB_T):
        r0 = sub * SUB_T
        x = x_ref[0, r0:r0 + SUB_T, :]
        hb = _rms_modulate(x, npre_ref[...], mod_ref[0, 1:2, :], mod_ref[0, 0:1, :]).astype(bf16)

        pn = _dot(hb, win_ref[:, 0:2 * D_MODEL])
        z_ssm = _dot(hb, win_ref[:, 3 * D_MODEL:4 * D_MODEL])
        gates = _dot(hb, win_ref[:, 4 * D_MODEL:6 * D_MODEL])
        u_pool = pn[:, 0:D_MODEL]

        pos = (j * BLOCK_T + r0 + 1
               + lax.broadcasted_iota(jnp.int32, (SUB_T, 1), 0)).astype(jnp.float32)
        mixed = []
        for g, w in enumerate(POOL_WINDOWS):
            cols = slice(g * POOL_GROUP_WIDTH, (g + 1) * POOL_GROUP_WIDTH)
            acc = jnp.concatenate([hist_ref[:, cols], u_pool[:, cols]], axis=0)
            span = 1
            while span < w:
                acc = acc + pltpu.roll(acc, span, 0)
                span *= 2
            inv_count = 1.0 / jnp.minimum(pos, float(w))
            pooled = acc[POOL_HALO:] * inv_count - u_pool[:, cols]
            mixed.append(_dot(pooled.astype(bf16), poolw_ref[g]))
        hist_ref[...] = u_pool[SUB_T - POOL_HALO:, :]
        y_pool = (jnp.concatenate(mixed, axis=-1) * pscale_ref[...]
                  * _silu_of_half(pn[:, D_MODEL:2 * D_MODEL]))

        chunk_rows = range(r0 // CHUNK, (r0 + SUB_T) // CHUNK, SUBLANES)
        tile = lambda k: slice(k * LANES, (k + 1) * LANES)
        for v in range(N_SLABS):
            pairs = [v * PAIRS_PER_SLAB + m for m in range(PAIRS_PER_SLAB)]
            for c0 in chunk_rows:
                rows = slice(c0, c0 + SUBLANES)
                for q in range(CHUNK // QUAD):
                    out = _swap_stage([yp_ref[0, p, rows, tile(q)] for p in pairs], 2)
                    for p, o in zip(pairs, out):
                        stage_ref[p, rows, tile(q)] = o
        for v in range(N_SLABS):
            pairs = [v * PAIRS_PER_SLAB + m for m in range(PAIRS_PER_SLAB)]
            for c0 in chunk_rows:
                rows = slice(c0, c0 + SUBLANES)
                for q in range(CHUNK // QUAD):
                    out = _swap_stage([stage_ref[p, rows, tile(q)] for p in pairs], 1)
                    for i in range(QUAD):
                        t = q * QUAD + i
                        ytok_ref[v, pl.ds(c0 * CHUNK + t, SUBLANES, stride=CHUNK), :] = (
                            out[i] + dskip_ref[v, :, tile(t)] * u_ref[0, v, rows, tile(t)])

        y = _gelu_tanh(jnp.concatenate([ytok_ref[v, r0:r0 + SUB_T, :] for v in range(N_SLABS)],
                                       axis=-1))
        y = y * _sigmoid_of_half(_dot(y.astype(bf16), gluw_ref[...]) + glub_ref[...])
        y_ssm = (y * _silu_of_half(z_ssm)).astype(bf16)

        merged = (_sigmoid_of_half(gates[:, 0:D_MODEL]) * _dot(y_pool.astype(bf16), wbp_ref[...])
                  + _sigmoid_of_half(gates[:, D_MODEL:2 * D_MODEL]) * _dot(y_ssm, wbs_ref[...]))
        out = _dot(merged.astype(bf16), wout_ref[...])
        rn = out * lax.rsqrt(jnp.mean(out * out, axis=-1, keepdims=True) + RMS_EPS) * npost_ref[...]
        o_ref[0, r0:r0 + SUB_T, :] = x + gate * rn


def _layer(x, c, w_ada, b_ada, norm_pre, norm_post, w_in, pool_w, pool_scale, a_re, a_im, log_dt,
           b_re, b_im, c_re, c_im, d_skip, glu_w, glu_b, w_branch_pool, w_branch_ssm, w_out):
    bsz, seq, d = x.shape
    assert d == D_MODEL and seq % BLOCK_T == 0
    bf16 = jnp.bfloat16
    f32 = jnp.float32

    toep, bst, cstt, pow_tab = _ssm_prep(a_re, a_im, log_dt, b_re, b_im, c_re, c_im)
    c_pad = jnp.zeros((SUBLANES, d), f32).at[:bsz].set(c)
    mod = _adaln(c_pad, w_ada, b_ada)[:bsz].reshape(bsz, 3, d)
    gate_cols = np.repeat(np.array([1.0, 0.5, 1.0, 0.5, 0.5, 0.5], np.float32), d)
    w_in_b = (w_in * gate_cols).astype(bf16)

    u_rows, u_pairs = _ssm_in(x, mod, norm_pre, w_in_b[:, 2 * d:3 * d])
    y_pairs = _ssm_core(u_pairs, toep, bst, cstt, pow_tab)
    d_rows = jnp.tile(d_skip.reshape(N_SLABS, 1, LANES), (1, SUBLANES, CHUNK))

    row = lambda v: v.reshape(1, d)
    tok_spec = pl.BlockSpec((1, BLOCK_T, d), lambda b, j: (b, j, 0))
    block_chunks = BLOCK_T // CHUNK
    u_spec = pl.BlockSpec((1, N_SLABS, block_chunks, SLAB_IN), lambda b, j: (b, 0, j, 0))
    y_spec = pl.BlockSpec((1, N_SLABS * PAIRS_PER_SLAB, block_chunks, PAIR_IN), lambda b, j: (b, 0, j, 0))
    operands = [
        (x, tok_spec),
        (mod, pl.BlockSpec((1, 3, d), lambda b, j: (b, 0, 0))),
        (row(norm_pre), None), (row(norm_post), None),
        (w_in_b, None),
        (u_rows, u_spec), (y_pairs, y_spec), (d_rows, None),
        (pool_w.astype(bf16), None), (row(pool_scale), None),
        ((0.5 * glu_w).astype(bf16), None), (row(0.5 * glu_b), None),
        (w_branch_pool.astype(bf16), None), (w_branch_ssm.astype(bf16), None),
        (w_out.astype(bf16), None),
    ]
    arrays = [a for a, _ in operands]
    specs = [s if s is not None else _const_spec(a.shape) for a, s in operands]

    return pl.pallas_call(
        _block_kernel,
        grid=(bsz, seq // BLOCK_T),
        in_specs=specs,
        out_specs=tok_spec,
        out_shape=jax.ShapeDtypeStruct((bsz, seq, d), x.dtype),
        scratch_shapes=[
            pltpu.VMEM((POOL_HALO, d), f32),
            pltpu.VMEM((N_SLABS * PAIRS_PER_SLAB, BLOCK_T // CHUNK, PAIR_IN), f32),
            pltpu.VMEM((N_SLABS, BLOCK_T, LANES), f32),
        ],
        compiler_params=pltpu.CompilerParams(
            dimension_semantics=("arbitrary", "arbitrary"),
            vmem_limit_bytes=VMEM_LIMIT_BYTES),
        name="block",
    )(*arrays)


def kernel(x, c, w_ada, b_ada, norm_pre, norm_post, w_in, pool_w, pool_scale, ssm_a_re, ssm_a_im,
           ssm_log_dt, ssm_b_re, ssm_b_im, ssm_c_re, ssm_c_im, ssm_d, glu_w, glu_b, w_branch_pool,
           w_branch_ssm, w_out):
    for layer in range(w_in.shape[0]):
        x = _layer(x, c, w_ada[layer], b_ada[layer], norm_pre[layer], norm_post[layer],
                   w_in[layer], pool_w[layer], pool_scale[layer], ssm_a_re[layer],
                   ssm_a_im[layer], ssm_log_dt[layer], ssm_b_re[layer], ssm_b_im[layer],
                   ssm_c_re[layer], ssm_c_im[layer], ssm_d.reshape(ssm_d.shape[0], -1)[layer],
                   glu_w[layer], glu_b[layer], w_branch_pool[layer], w_branch_ssm[layer],
                   w_out[layer])
    return x
```
